```python
import jax, jax.numpy as jnp
from jax import lax
import numpy as np

D_MODEL = 1024
BATCH = 2
SEQ = 16384
DEPTH = 2

GRID_W = 64
CTX_LEN = 256
Q_BLOCK = 128
ROPE_THETA = 10000.0
RMS_EPS = 1e-6
N_MOD = 6

GQA_HEADS = 8
GQA_KV_HEADS = 2
GQA_GROUP = GQA_HEADS // GQA_KV_HEADS
GQA_HEAD_DIM = 64
GQA_Q_WIDTH = GQA_HEADS * GQA_HEAD_DIM
GQA_KV_WIDTH = GQA_KV_HEADS * GQA_HEAD_DIM

MLA_HEADS = 8
MLA_NOPE_DIM = 64
MLA_ROPE_DIM = 32
MLA_V_DIM = 64
MLA_Q_RANK = 256
MLA_KV_RANK = 128
MLA_V_WIDTH = MLA_HEADS * MLA_V_DIM

GMLP_GROUPS = 8
GMLP_GROUP_DIM = 64
GMLP_CHUNK = 128
GMLP_WIDTH = GMLP_GROUPS * GMLP_GROUP_DIM

N_BRANCHES = 3
BRANCH_WIDTH = 512

IN_SPLITS = (GQA_Q_WIDTH, GQA_KV_WIDTH, GQA_KV_WIDTH, MLA_Q_RANK, MLA_KV_RANK, MLA_ROPE_DIM,
             2 * GMLP_WIDTH, N_BRANCHES * D_MODEL)
IN_WIDTH = (GQA_Q_WIDTH + 2 * GQA_KV_WIDTH + MLA_Q_RANK + MLA_KV_RANK + MLA_ROPE_DIM
            + 2 * GMLP_WIDTH + N_BRANCHES * D_MODEL)

N_EXPERTS = 32
TOP_K = 4
D_EXPERT = 1024
SWIGLU_ALPHA = 1.702
SWIGLU_LIMIT = 7.0
MOE_BLOCK = 256

kernel_name = "hybrid_gqa_mla_gmlp_moe_dit_block"


def rms_norm(x, gain):
    xf = x.astype(jnp.float32)
    y = xf * lax.rsqrt(jnp.mean(xf * xf, axis=-1, keepdims=True) + RMS_EPS)
    return (y * gain.astype(jnp.float32)).astype(x.dtype)


def modulate(x, gain, shift, scale):
    return rms_norm(x, gain) * (1 + scale) + shift


def axial_rope_angles(rows, rot_dim):
    row = jnp.repeat(jnp.arange(rows, dtype=jnp.float32), GRID_W)
    col = jnp.tile(jnp.arange(GRID_W, dtype=jnp.float32), rows)
    quarter = rot_dim // 4
    inv_freq = ROPE_THETA ** (-jnp.arange(quarter, dtype=jnp.float32) / quarter)
    return jnp.concatenate([row[:, None] * inv_freq, col[:, None] * inv_freq], axis=-1)


def apply_rope(x, ang):
    half = x.shape[-1] // 2
    cos = jnp.cos(ang)[:, None, :]
    sin = jnp.sin(ang)[:, None, :]
    xf = x.astype(jnp.float32)
    x1, x2 = xf[..., :half], xf[..., half:]
    return jnp.concatenate([x1 * cos - x2 * sin, x2 * cos + x1 * sin], axis=-1).astype(x.dtype)


def sweep_query_blocks(fn, *qs):
    B, L = qs[0].shape[:2]
    nb = L // Q_BLOCK
    blocks = tuple(jnp.moveaxis(q.reshape(B, nb, Q_BLOCK, *q.shape[2:]), 1, 0) for q in qs)
    out = lax.map(lambda blk: fn(*blk), blocks)
    return jnp.moveaxis(out, 0, 1).reshape(B, L, *out.shape[3:])


def gqa_attend(q, k, v):
    s = jnp.einsum('bqhgd,bkhd->bhgqk', q, k, preferred_element_type=jnp.float32) * (GQA_HEAD_DIM ** -0.5)
    p = jax.nn.softmax(s, axis=-1).astype(v.dtype)
    return jnp.einsum('bhgqk,bkhd->bqhgd', p, v)


def mla_attend(q_nope, q_rope, k_nope, k_rope, v):
    s = (jnp.einsum('bqhd,bkhd->bhqk', q_nope, k_nope, preferred_element_type=jnp.float32)
         + jnp.einsum('bqhd,bkd->bhqk', q_rope, k_rope, preferred_element_type=jnp.float32))
    s = s * ((MLA_NOPE_DIM + MLA_ROPE_DIM) ** -0.5)
    p = jax.nn.softmax(s, axis=-1).astype(v.dtype)
    return jnp.einsum('bhqk,bkhd->bqhd', p, v)


def spatial_gate(u, v, v_norm, w_s, b_s):
    B, L, _ = v.shape
    vc = rms_norm(v, v_norm).reshape(B, L // GMLP_CHUNK, GMLP_CHUNK, GMLP_GROUPS, GMLP_GROUP_DIM)
    s = jnp.einsum('gpq,bcqgd->bcpgd', w_s, vc) + b_s.T[:, :, None]
    return u * s.reshape(B, L, GMLP_WIDTH)


def project_stream(h, w_in, qk_norm, mla_q_norm, mla_kv_norm, mla_w_uq, mla_w_ukv):
    B, L, _ = h.shape
    z = h @ w_in
    points = np.cumsum(IN_SPLITS)[:-1].tolist()
    q, k, v, cq, ckv, kr, uv, gates = jnp.split(z, points, axis=-1)
    out = {}
    out['q'] = rms_norm(q.reshape(B, L, GQA_HEADS, GQA_HEAD_DIM), qk_norm[0])
    out['k'] = rms_norm(k.reshape(B, L, GQA_KV_HEADS, GQA_HEAD_DIM), qk_norm[1])
    out['v'] = v.reshape(B, L, GQA_KV_HEADS, GQA_HEAD_DIM)
    qm = (rms_norm(cq, mla_q_norm) @ mla_w_uq).reshape(B, L, MLA_HEADS, MLA_NOPE_DIM + MLA_ROPE_DIM)
    out['q_nope'], out['q_rope'] = qm[..., :MLA_NOPE_DIM], qm[..., MLA_NOPE_DIM:]
    kv = (rms_norm(ckv, mla_kv_norm) @ mla_w_ukv).reshape(B, L, MLA_HEADS, MLA_NOPE_DIM + MLA_V_DIM)
    out['k_nope'], out['v_mla'] = kv[..., :MLA_NOPE_DIM], kv[..., MLA_NOPE_DIM:]
    out['k_rope'] = kr
    act = jax.nn.gelu(uv)
    out['u'], out['v_gmlp'] = act[..., :GMLP_WIDTH], act[..., GMLP_WIDTH:]
    out['gates'] = gates.reshape(B, L, N_BRANCHES, D_MODEL)
    return out


def branch_merge(outs, gates, w_branch, w_out):
    merged = jax.nn.sigmoid(gates[:, :, 0]) * (outs[0] @ w_branch[0])
    for i in range(1, N_BRANCHES):
        merged = merged + jax.nn.sigmoid(gates[:, :, i]) * (outs[i] @ w_branch[i])
    return merged @ w_out


def moe_ffn(tokens, router_w, router_b, w1, b1, w2, b2):
    T, D = tokens.shape
    logits = jnp.matmul(tokens, router_w, preferred_element_type=jnp.float32) + router_b.astype(jnp.float32)
    top_logit, top_idx = lax.top_k(logits, TOP_K)
    top_w = jax.nn.softmax(top_logit, axis=-1)
    P = T * TOP_K
    e_flat = top_idx.reshape(P)
    tok_flat = jnp.repeat(jnp.arange(T, dtype=jnp.int32), TOP_K)
    w_flat = top_w.reshape(P)
    order = jnp.argsort(e_flat, stable=True)
    e_s, tok_s, w_s = e_flat[order], tok_flat[order], w_flat[order]
    counts = jnp.zeros((N_EXPERTS,), jnp.int32).at[e_flat].add(1)
    padded = (counts + MOE_BLOCK - 1) // MOE_BLOCK * MOE_BLOCK
    pad_end = jnp.cumsum(padded)
    pad_start = pad_end - padded
    grp_start = jnp.cumsum(counts) - counts
    dest = pad_start[e_s] + (jnp.arange(P, dtype=jnp.int32) - grp_start[e_s])
    n_blocks = -(-(P + N_EXPERTS * (MOE_BLOCK - 1)) // MOE_BLOCK)
    buf = jnp.zeros((n_blocks * MOE_BLOCK, D), tokens.dtype).at[dest].set(tokens[tok_s])
    blk_start = jnp.arange(n_blocks, dtype=jnp.int32) * MOE_BLOCK
    blk_expert = jnp.minimum(jnp.searchsorted(pad_end, blk_start, side='right'), N_EXPERTS - 1)

    def run_block(args):
        xb, e = args
        a = xb @ w1[e] + b1[e]
        glu = jnp.minimum(a[..., :D_EXPERT], SWIGLU_LIMIT)
        lin = jnp.clip(a[..., D_EXPERT:], -SWIGLU_LIMIT, SWIGLU_LIMIT)
        act = glu * jax.nn.sigmoid(SWIGLU_ALPHA * glu) * (lin + 1)
        return act @ w2[e] + b2[e]

    y = lax.map(run_block, (buf.reshape(n_blocks, MOE_BLOCK, D), blk_expert)).reshape(n_blocks * MOE_BLOCK, D)
    contrib = y[dest] * w_s[:, None].astype(y.dtype)
    return jax.ops.segment_sum(contrib, tok_s, num_segments=T)


def hybrid_layer(x, xc, mod, mod_c, ang_head, ang_mla, params, update_ctx):
    (norms, w_in, qk_norm, mla_q_norm, mla_kv_norm, mla_w_uq, mla_w_ukv, gmlp_v_norm, gmlp_w_s,
     gmlp_b_s, w_branch, w_out, router_w, router_b, ew1, eb1, ew2, eb2) = params
    B, L, _ = x.shape
    C = xc.shape[1]
    sh1, sc1, g1, sh2, sc2, g2 = [mod[:, j, None, :] for j in range(N_MOD)]
    sh1c, sc1c, g1c, sh2c, sc2c, g2c = [mod_c[j] for j in range(N_MOD)]

    lat = project_stream(modulate(x, norms[0], sh1, sc1), w_in, qk_norm, mla_q_norm, mla_kv_norm, mla_w_uq, mla_w_ukv)
    cx = project_stream(modulate(xc, norms[0], sh1c, sc1c), w_in, qk_norm, mla_q_norm, mla_kv_norm, mla_w_uq, mla_w_ukv)

    q = apply_rope(lat['q'], ang_head)
    k_all = jnp.concatenate([apply_rope(lat['k'], ang_head), cx['k']], axis=1)
    v_all = jnp.concatenate([lat['v'], cx['v']], axis=1)
    gqa_o = sweep_query_blocks(lambda qb: gqa_attend(qb, k_all, v_all),
                               q.reshape(B, L, GQA_KV_HEADS, GQA_GROUP, GQA_HEAD_DIM)).reshape(B, L, GQA_Q_WIDTH)

    q_rope = apply_rope(lat['q_rope'], ang_mla)
    k_rope_lat = apply_rope(lat['k_rope'][:, :, None, :], ang_mla)[:, :, 0, :]
    kn_all = jnp.concatenate([lat['k_nope'], cx['k_nope']], axis=1)
    kr_all = jnp.concatenate([k_rope_lat, cx['k_rope']], axis=1)
    vm_all = jnp.concatenate([lat['v_mla'], cx['v_mla']], axis=1)
    mla_o = sweep_query_blocks(lambda qn, qr: mla_attend(qn, qr, kn_all, kr_all, vm_all),
                               lat['q_nope'], q_rope).reshape(B, L, MLA_V_WIDTH)

    gmlp_o = spatial_gate(lat['u'], lat['v_gmlp'], gmlp_v_norm, gmlp_w_s, gmlp_b_s)

    y = branch_merge([gqa_o, mla_o, gmlp_o], lat['gates'], w_branch, w_out)
    x_new = x + g1 * rms_norm(y, norms[1])

    if update_ctx:
        gqa_c = gqa_attend(cx['q'].reshape(B, C, GQA_KV_HEADS, GQA_GROUP, GQA_HEAD_DIM), cx['k'], cx['v']).reshape(B, C, GQA_Q_WIDTH)
        mla_c = mla_attend(cx['q_nope'], cx['q_rope'], cx['k_nope'], cx['k_rope'], cx['v_mla']).reshape(B, C, MLA_V_WIDTH)
        gmlp_c = spatial_gate(cx['u'], cx['v_gmlp'], gmlp_v_norm, gmlp_w_s, gmlp_b_s)
        yc = branch_merge([gqa_c, mla_c, gmlp_c], cx['gates'], w_branch, w_out)
        xc = xc + g1c * rms_norm(yc, norms[1])
    x = x_new

    tokens = modulate(x, norms[2], sh2, sc2).reshape(B * L, D_MODEL)
    if update_ctx:
        tokens = jnp.concatenate([tokens, modulate(xc, norms[2], sh2c, sc2c).reshape(B * C, D_MODEL)], axis=0)
    f = moe_ffn(tokens, router_w, router_b, ew1, eb1, ew2, eb2)
    x = x + g2 * rms_norm(f[:B * L].reshape(B, L, D_MODEL), norms[3])
    if update_ctx:
        xc = xc + g2c * rms_norm(f[B * L:].reshape(B, C, D_MODEL), norms[3])
    return x, xc


def setup_inputs(seed: int = 0) -> dict:
    key = jax.random.key(seed)
    ks = jax.random.split(key, 32)

    def nrm(k, shape, scale):
        return scale * jax.random.normal(k, shape, jnp.float32)

    D = D_MODEL
    return {
        "x": nrm(ks[0], (BATCH, SEQ, D), 1.0),
        "c": nrm(ks[1], (BATCH, D), 1.0),
        "ctx": nrm(ks[2], (BATCH, CTX_LEN, D), 1.0),
        "c_ctx": nrm(ks[3], (D,), 1.0),
        "w_mod": nrm(ks[4], (DEPTH, D, N_MOD * D), 0.5 * D ** -0.5),
        "b_mod": nrm(ks[5], (DEPTH, N_MOD * D), 0.02),
        "norm_gains": 1.0 + nrm(ks[6], (DEPTH, 4, D), 0.05),
        "w_in": nrm(ks[7], (DEPTH, D, IN_WIDTH), D ** -0.5),
        "qk_norm": 1.0 + nrm(ks[8], (DEPTH, 2, GQA_HEAD_DIM), 0.05),
        "mla_q_norm": 1.0 + nrm(ks[9], (DEPTH, MLA_Q_RANK), 0.05),
        "mla_kv_norm": 1.0 + nrm(ks[10], (DEPTH, MLA_KV_RANK), 0.05),
        "mla_w_uq": nrm(ks[11], (DEPTH, MLA_Q_RANK, MLA_HEADS * (MLA_NOPE_DIM + MLA_ROPE_DIM)), MLA_Q_RANK ** -0.5),
        "mla_w_ukv": nrm(ks[12], (DEPTH, MLA_KV_RANK, MLA_HEADS * (MLA_NOPE_DIM + MLA_V_DIM)), MLA_KV_RANK ** -0.5),
        "gmlp_v_norm": 1.0 + nrm(ks[13], (DEPTH, GMLP_WIDTH), 0.05),
        "gmlp_w_s": nrm(ks[14], (DEPTH, GMLP_GROUPS, GMLP_CHUNK, GMLP_CHUNK), GMLP_CHUNK ** -0.5),
        "gmlp_b_s": nrm(ks[15], (DEPTH, GMLP_GROUPS, GMLP_CHUNK), 0.02),
        "w_branch": nrm(ks[16], (DEPTH, N_BRANCHES, BRANCH_WIDTH, D), BRANCH_WIDTH ** -0.5),
        "w_out": nrm(ks[17], (DEPTH, D, D), D ** -0.5),
        "router_w": nrm(ks[18], (DEPTH, D, N_EXPERTS), D ** -0.5),
        "router_b": nrm(ks[19], (DEPTH, N_EXPERTS), 0.01),
        "expert_w1": nrm(ks[20], (DEPTH, N_EXPERTS, D, 2 * D_EXPERT), D ** -0.5),
        "expert_b1": nrm(ks[21], (DEPTH, N_EXPERTS, 2 * D_EXPERT), 0.02),
        "expert_w2": nrm(ks[22], (DEPTH, N_EXPERTS, D_EXPERT, D), D_EXPERT ** -0.5),
        "expert_b2": nrm(ks[23], (DEPTH, N_EXPERTS, D), 0.02),
    }


def reference(x, c, ctx, c_ctx, w_mod, b_mod, norm_gains, w_in, qk_norm, mla_q_norm, mla_kv_norm,
              mla_w_uq, mla_w_ukv, gmlp_v_norm, gmlp_w_s, gmlp_b_s, w_branch, w_out, router_w, router_b,
              expert_w1, expert_b1, expert_w2, expert_b2):
    n_lat = x.shape[1]
    rows = n_lat // GRID_W
    ang_head = axial_rope_angles(rows, GQA_HEAD_DIM)
    ang_mla = axial_rope_angles(rows, MLA_ROPE_DIM)
    s_c = jax.nn.silu(c)
    s_cc = jax.nn.silu(c_ctx)
    xc = ctx
    for i in range(DEPTH):
        mod = (s_c @ w_mod[i] + b_mod[i]).reshape(-1, N_MOD, D_MODEL)
        mod_c = (s_cc @ w_mod[i] + b_mod[i]).reshape(N_MOD, D_MODEL)
        params = (norm_gains[i], w_in[i], qk_norm[i], mla_q_norm[i], mla_kv_norm[i], mla_w_uq[i], mla_w_ukv[i],
                  gmlp_v_norm[i], gmlp_w_s[i], gmlp_b_s[i], w_branch[i], w_out[i], router_w[i], router_b[i],
                  expert_w1[i], expert_b1[i], expert_w2[i], expert_b2[i])
        x, xc = hybrid_layer(x, xc, mod, mod_c, ang_head, ang_mla, params, update_ctx=(i < DEPTH - 1))
    return x
```

```python
import functools
import math

import jax
import jax.numpy as jnp
import numpy as np
from jax import lax
from jax.experimental import pallas as pl
from jax.experimental.pallas import tpu as pltpu

GRID_W = 64
ROPE_THETA = 10000.0
RMS_EPS = 1e-6
N_MOD = 6
GQA_HEADS = 8
GQA_KV_HEADS = 2
GQA_HEAD_DIM = 64
MLA_HEADS = 8
MLA_NOPE_DIM = 64
MLA_ROPE_DIM = 32
MLA_V_DIM = 64
MLA_Q_RANK = 256
MLA_KV_RANK = 128
GMLP_GROUPS = 8
GMLP_GROUP_DIM = 64
GMLP_CHUNK = 128
N_BRANCHES = 3
TOP_K = 4
SWIGLU_ALPHA = 1.702
SWIGLU_LIMIT = 7.0

LANES = 128
VMEM_LIMIT_BYTES = 56 * 2**20
NEG_BIG = -1e30

BF16 = jnp.bfloat16
F32 = jnp.float32

GQA_Q_WIDTH = GQA_HEADS * GQA_HEAD_DIM
GQA_KV_WIDTH = GQA_KV_HEADS * GQA_HEAD_DIM
GMLP_WIDTH = GMLP_GROUPS * GMLP_GROUP_DIM
MLA_QN_WIDTH = MLA_HEADS * MLA_NOPE_DIM
MLA_QR_WIDTH = MLA_HEADS * MLA_ROPE_DIM
MLA_V_WIDTH = MLA_HEADS * MLA_V_DIM


def _pick_tile(n, candidates):
    for t in candidates:
        if n % t == 0:
            return t
    raise ValueError(f"no tile in {candidates} divides {n}")


def _full_spec(a):
    nd = a.ndim
    return pl.BlockSpec(a.shape, lambda *_: (0,) * nd)


def _row_rms(x, gain_row):
    ms = jnp.mean(x * x, axis=-1, keepdims=True)
    return x * lax.rsqrt(ms + RMS_EPS) * gain_row


def _group_rms(x, group_mean_mat, gain_row):
    ms = jnp.dot((x * x).astype(BF16), group_mean_mat, preferred_element_type=F32)
    return x * lax.rsqrt(ms + RMS_EPS) * gain_row


def _rope_lanes(x, cos_t, sin_t, half):
    rows, n = x.shape
    lane = lax.broadcasted_iota(jnp.int32, (rows, LANES), 1)
    first_half = (lane % (2 * half)) < half
    out = []
    for j in range(n // LANES):
        xb = x[:, j * LANES:(j + 1) * LANES]
        partner_up = pltpu.roll(xb, LANES - half, 1)
        partner_dn = pltpu.roll(xb, half, 1)
        partner = jnp.where(first_half, partner_up, partner_dn)
        out.append(xb * cos_t + partner * sin_t)
    return out[0] if len(out) == 1 else jnp.concatenate(out, axis=1)


def _proj_kernel(x_ref, mod_ref, rope_ref, small_ref,
                 wq_ref, wk_ref, wv_ref, wcq_ref, wckv_ref, wkr_ref, wuv_ref, wg_ref,
                 wuq_ref, wukv_ref, gq_ref, gk_ref,
                 q_ref, k_ref, v_ref, qn_ref, qr_ref, kn_ref, vm_ref, kr_ref, u_ref, vg_ref, g_ref):
    x = x_ref[0]
    mod = mod_ref[0]
    h = _row_rms(x, mod[0:1]) * mod[1:2] + mod[2:3]
    hb = h.astype(BF16)

    cos64, sin64 = rope_ref[0], rope_ref[1]
    cos32, sin32 = rope_ref[2], rope_ref[3]
    cos_kr, sin_kr = rope_ref[4], rope_ref[5]

    small = small_ref[...]
    qn_gain = small[0:1, :GQA_Q_WIDTH]
    kn_gain = small[1:2, :GQA_KV_WIDTH]
    cq_gain = small[2:3, :MLA_Q_RANK]
    ckv_gain = small[3:4, :MLA_KV_RANK]
    vg_gain = small[4:5, :GMLP_WIDTH]

    def proj(w_ref):
        return jnp.dot(hb, w_ref[...], preferred_element_type=F32)

    q = _group_rms(proj(wq_ref), gq_ref[...], qn_gain)
    q = _rope_lanes(q, cos64, sin64, GQA_HEAD_DIM // 2) * (GQA_HEAD_DIM ** -0.5)
    q_ref[0] = q.astype(BF16)
    k = _group_rms(proj(wk_ref), gk_ref[...], kn_gain)
    k_ref[0] = _rope_lanes(k, cos64, sin64, GQA_HEAD_DIM // 2).astype(BF16)
    v_ref[0] = proj(wv_ref).astype(BF16)

    cq = _row_rms(proj(wcq_ref), cq_gain).astype(BF16)
    qm = jnp.dot(cq, wuq_ref[...], preferred_element_type=F32)
    mla_scale = (MLA_NOPE_DIM + MLA_ROPE_DIM) ** -0.5
    qn_ref[0] = (qm[:, :MLA_QN_WIDTH] * mla_scale).astype(BF16)
    qr = _rope_lanes(qm[:, MLA_QN_WIDTH:], cos32, sin32, MLA_ROPE_DIM // 2) * mla_scale
    qr_ref[0] = qr.astype(BF16)
    ckv = _row_rms(proj(wckv_ref), ckv_gain).astype(BF16)
    kv = jnp.dot(ckv, wukv_ref[...], preferred_element_type=F32)
    kn_ref[0] = kv[:, :MLA_QN_WIDTH].astype(BF16)
    vm_ref[0] = kv[:, MLA_QN_WIDTH:].astype(BF16)
    kr_ref[0] = _rope_lanes(proj(wkr_ref), cos_kr, sin_kr, MLA_ROPE_DIM // 2).astype(BF16)

    act = jax.nn.gelu(proj(wuv_ref), approximate=True)
    u_ref[0] = act[:, :GMLP_WIDTH].astype(BF16)
    vg_ref[0] = _row_rms(act[:, GMLP_WIDTH:], vg_gain).astype(BF16)

    g_ref[0] = proj(wg_ref).astype(BF16)


def _proj_call(x, mod3, rope, small, weights, tm):
    B, L, D = x.shape
    widths = (GQA_Q_WIDTH, GQA_KV_WIDTH, GQA_KV_WIDTH, MLA_QN_WIDTH, MLA_QR_WIDTH, MLA_QN_WIDTH,
              MLA_V_WIDTH, LANES, GMLP_WIDTH, GMLP_WIDTH, N_BRANCHES * D)
    tok_spec = lambda w: pl.BlockSpec((1, tm, w), lambda b, i: (b, i, 0))
    in_specs = [tok_spec(D),
                pl.BlockSpec((1,) + mod3.shape[1:], lambda b, i: (b, 0, 0)),
                pl.BlockSpec((rope.shape[0], tm, LANES), lambda b, i: (0, i, 0)),
                _full_spec(small)] + [_full_spec(w) for w in weights]
    return pl.pallas_call(
        _proj_kernel,
        grid=(B, L // tm),
        in_specs=in_specs,
        out_specs=[tok_spec(w) for w in widths],
        out_shape=[jax.ShapeDtypeStruct((B, L, w), BF16) for w in widths],
        compiler_params=pltpu.CompilerParams(
            dimension_semantics=("parallel", "parallel"), vmem_limit_bytes=VMEM_LIMIT_BYTES),
        name="proj",
    )(x, mod3, rope, small, *weights)


def _attn_kernel(q_ref, k_ref, v_ref, o_ref, m_ref, l_ref, acc_ref):
    ki = pl.program_id(3)

    @pl.when(ki == 0)
    def _():
        m_ref[...] = jnp.full(m_ref.shape, NEG_BIG, F32)
        l_ref[...] = jnp.zeros(l_ref.shape, F32)
        acc_ref[...] = jnp.zeros(acc_ref.shape, F32)

    s = lax.dot_general(q_ref[0, 0], k_ref[0, 0], (((1,), (1,)), ((), ())),
                        preferred_element_type=F32)
    m_prev = m_ref[...]
    m_new = jnp.maximum(m_prev, jnp.max(s, axis=1, keepdims=True))
    alpha = jnp.exp(m_prev - m_new)
    p = jnp.exp(s - m_new)
    l_ref[...] = alpha * l_ref[...] + jnp.sum(p, axis=1, keepdims=True)
    acc_ref[...] = alpha * acc_ref[...] + jnp.dot(p.astype(BF16), v_ref[0, 0],
                                                  preferred_element_type=F32)
    m_ref[...] = m_new

    @pl.when(ki == pl.num_programs(3) - 1)
    def _():
        o_ref[0, 0] = (acc_ref[...] / l_ref[...]).astype(o_ref.dtype)


def _attn_call(q, k, v, k_head_of, v_head_of, tq, tk):
    B, H, Lq, _ = q.shape
    Lk = k.shape[2]
    return pl.pallas_call(
        _attn_kernel,
        grid=(B, H, Lq // tq, Lk // tk),
        in_specs=[pl.BlockSpec((1, 1, tq, LANES), lambda b, h, i, j: (b, h, i, 0)),
                  pl.BlockSpec((1, 1, tk, LANES), lambda b, h, i, j: (b, k_head_of(h), j, 0)),
                  pl.BlockSpec((1, 1, tk, LANES), lambda b, h, i, j: (b, v_head_of(h), j, 0))],
        out_specs=pl.BlockSpec((1, 1, tq, LANES), lambda b, h, i, j: (b, h, i, 0)),
        out_shape=jax.ShapeDtypeStruct((B, H, Lq, LANES), BF16),
        scratch_shapes=[pltpu.VMEM((tq, 1), F32), pltpu.VMEM((tq, 1), F32),
                        pltpu.VMEM((tq, LANES), F32)],
        compiler_params=pltpu.CompilerParams(
            dimension_semantics=("parallel", "parallel", "parallel", "arbitrary"),
            vmem_limit_bytes=VMEM_LIMIT_BYTES),
        name="attn",
    )(q, k, v)


def _merge_kernel(x_ref, mod_ref, ga_ref, ma_ref, u_ref, vg_ref, g_ref,
                  ws_ref, bs_ref, wb_ref, wo_ref, rw_ref, rb_ref,
                  xo_ref, tok_ref, lg_ref):
    tm = x_ref.shape[1]
    D = x_ref.shape[2]
    mod = mod_ref[0]

    lane = lax.broadcasted_iota(jnp.int32, (GMLP_CHUNK, LANES), 1)
    low_group = lane < GMLP_GROUP_DIM
    chunks = []
    for c in range(tm // GMLP_CHUNK):
        rows = slice(c * GMLP_CHUNK, (c + 1) * GMLP_CHUNK)
        cols = []
        for j in range(GMLP_WIDTH // LANES):
            vc = vg_ref[0, rows, j * LANES:(j + 1) * LANES]
            s_lo = jnp.dot(ws_ref[2 * j], vc, preferred_element_type=F32)
            s_hi = jnp.dot(ws_ref[2 * j + 1], vc, preferred_element_type=F32)
            cols.append(jnp.where(low_group, s_lo, s_hi))
        s = jnp.concatenate(cols, axis=1) + bs_ref[...]
        chunks.append((u_ref[0, rows, :].astype(F32) * s).astype(BF16))
    gm = chunks[0] if len(chunks) == 1 else jnp.concatenate(chunks, axis=0)

    branches = (ga_ref[0], ma_ref[0], gm)
    merged = None
    for i in range(N_BRANCHES):
        gate = jax.nn.sigmoid(g_ref[0, :, i * D:(i + 1) * D].astype(F32))
        term = gate * jnp.dot(branches[i], wb_ref[i], preferred_element_type=F32)
        merged = term if merged is None else merged + term
    y = jnp.dot(merged.astype(BF16), wo_ref[...], preferred_element_type=F32)
    x_new = x_ref[0] + mod[1:2] * _row_rms(y, mod[0:1])
    xo_ref[0] = x_new

    tok = _row_rms(x_new, mod[2:3]) * mod[3:4] + mod[4:5]
    tok_ref[0] = tok.astype(BF16)
    lg_ref[0] = jnp.dot(tok, rw_ref[...], preferred_element_type=F32,
                        precision=lax.Precision.HIGHEST) + rb_ref[...]


def _merge_call(x, mod5, ga, ma, u, vg, g, ws, bs, wb, wo, rw, rb, tm):
    B, L, D = x.shape
    tok_spec = lambda w: pl.BlockSpec((1, tm, w), lambda b, i: (b, i, 0))
    consts = (ws, bs, wb, wo, rw, rb)
    return pl.pallas_call(
        _merge_kernel,
        grid=(B, L // tm),
        in_specs=[tok_spec(D), pl.BlockSpec((1,) + mod5.shape[1:], lambda b, i: (b, 0, 0)),
                  tok_spec(GQA_Q_WIDTH), tok_spec(MLA_V_WIDTH), tok_spec(GMLP_WIDTH),
                  tok_spec(GMLP_WIDTH), tok_spec(N_BRANCHES * D)] + [_full_spec(a) for a in consts],
        out_specs=[tok_spec(D), tok_spec(D), tok_spec(LANES)],
        out_shape=[jax.ShapeDtypeStruct((B, L, D), F32), jax.ShapeDtypeStruct((B, L, D), BF16),
                   jax.ShapeDtypeStruct((B, L, LANES), F32)],
        compiler_params=pltpu.CompilerParams(
            dimension_semantics=("parallel", "parallel"), vmem_limit_bytes=VMEM_LIMIT_BYTES),
        name="merge",
    )(x, mod5, ga, ma, u, vg, g, *consts)


def _expert_kernel(blk_expert_ref, n_used_ref, xb_ref, w1_ref, b1_ref, w2_ref, b2_ref, y_ref):
    del blk_expert_ref
    d_expert = w2_ref.shape[1]

    @pl.when(pl.program_id(0) < n_used_ref[0])
    def _():
        a = jnp.dot(xb_ref[...], w1_ref[0], preferred_element_type=F32) + b1_ref[0]
        glu = jnp.minimum(a[:, :d_expert], SWIGLU_LIMIT)
        lin = jnp.clip(a[:, d_expert:], -SWIGLU_LIMIT, SWIGLU_LIMIT)
        act = glu * jax.nn.sigmoid(SWIGLU_ALPHA * glu) * (lin + 1.0)
        y_ref[...] = jnp.dot(act.astype(BF16), w2_ref[0], preferred_element_type=F32) + b2_ref[0]

    @pl.when(pl.program_id(0) >= n_used_ref[0])
    def _():
        y_ref[...] = jnp.zeros(y_ref.shape, y_ref.dtype)


def _expert_call(blk_expert, n_used, xb, w1, b1, w2, b2, rows_per_block):
    n_rows, D = xb.shape
    E, _, two_de = w1.shape
    d_expert = w2.shape[1]
    grid_spec = pltpu.PrefetchScalarGridSpec(
        num_scalar_prefetch=2,
        grid=(n_rows // rows_per_block,),
        in_specs=[pl.BlockSpec((rows_per_block, D), lambda i, be, nu: (i, 0)),
                  pl.BlockSpec((1, D, two_de), lambda i, be, nu: (be[i], 0, 0)),
                  pl.BlockSpec((1, 1, two_de), lambda i, be, nu: (be[i], 0, 0)),
                  pl.BlockSpec((1, d_expert, D), lambda i, be, nu: (be[i], 0, 0)),
                  pl.BlockSpec((1, 1, D), lambda i, be, nu: (be[i], 0, 0))],
        out_specs=pl.BlockSpec((rows_per_block, D), lambda i, be, nu: (i, 0)),
    )
    return pl.pallas_call(
        _expert_kernel,
        grid_spec=grid_spec,
        out_shape=jax.ShapeDtypeStruct((n_rows, D), F32),
        compiler_params=pltpu.CompilerParams(
            dimension_semantics=("arbitrary",), vmem_limit_bytes=VMEM_LIMIT_BYTES),
        name="experts",
    )(blk_expert, n_used, xb, w1, b1.reshape(E, 1, two_de), w2, b2.reshape(E, 1, D))


def _final_kernel(x_ref, f_ref, mod_ref, o_ref):
    mod = mod_ref[0]
    o_ref[0] = x_ref[0] + mod[1:2] * _row_rms(f_ref[0], mod[0:1])


def _final_call(x, f, mod2, tm):
    B, L, D = x.shape
    tok_spec = pl.BlockSpec((1, tm, D), lambda b, i: (b, i, 0))
    return pl.pallas_call(
        _final_kernel,
        grid=(B, L // tm),
        in_specs=[tok_spec, tok_spec, pl.BlockSpec((1,) + mod2.shape[1:], lambda b, i: (b, 0, 0))],
        out_specs=tok_spec,
        out_shape=jax.ShapeDtypeStruct((B, L, D), F32),
        compiler_params=pltpu.CompilerParams(
            dimension_semantics=("parallel", "parallel"), vmem_limit_bytes=VMEM_LIMIT_BYTES),
        name="final",
    )(x, f, mod2)


def _rope_tables(n_lat, dtype=F32):
    rows = n_lat // GRID_W
    row = jnp.repeat(jnp.arange(rows, dtype=F32), GRID_W)
    col = jnp.tile(jnp.arange(GRID_W, dtype=F32), rows)

    def tables(rot_dim):
        quarter = rot_dim // 4
        inv_freq = ROPE_THETA ** (-jnp.arange(quarter, dtype=F32) / quarter)
        ang = jnp.concatenate([row[:, None] * inv_freq, col[:, None] * inv_freq], axis=-1)
        cos = jnp.concatenate([jnp.cos(ang), jnp.cos(ang)], axis=-1)
        sin = jnp.concatenate([-jnp.sin(ang), jnp.sin(ang)], axis=-1)
        return cos, sin

    cos64, sin64 = tables(GQA_HEAD_DIM)
    cos32, sin32 = tables(MLA_ROPE_DIM)
    tile = lambda t: jnp.tile(t, (1, LANES // t.shape[1]))
    pad_one = lambda t: jnp.concatenate([t, jnp.ones((n_lat, LANES - t.shape[1]), F32)], axis=1)
    pad_zero = lambda t: jnp.concatenate([t, jnp.zeros((n_lat, LANES - t.shape[1]), F32)], axis=1)
    return jnp.stack([tile(cos64), tile(sin64), tile(cos32), tile(sin32),
                      pad_one(cos32), pad_zero(sin32)]).astype(dtype)


def _identity_rope(n):
    one, zero = jnp.ones((n, LANES), F32), jnp.zeros((n, LANES), F32)
    return jnp.stack([one, zero, one, zero, one, zero])


def _group_mean_matrix(width, group):
    idx = np.arange(width) // group
    return jnp.asarray((idx[:, None] == idx[None, :]).astype(np.float32) / group, dtype=BF16)


def _layer_weights(w_in, mla_w_uq, mla_w_ukv, D):
    splits = np.cumsum((GQA_Q_WIDTH, GQA_KV_WIDTH, GQA_KV_WIDTH, MLA_Q_RANK, MLA_KV_RANK, MLA_ROPE_DIM,
                        2 * GMLP_WIDTH, N_BRANCHES * D))[:-1].tolist()
    wq, wk, wv, wcq, wckv, wkr, wuv, wg = jnp.split(w_in.astype(BF16), splits, axis=1)
    wkr = jnp.pad(wkr, ((0, 0), (0, LANES - MLA_ROPE_DIM)))
    uq = mla_w_uq.astype(BF16).reshape(MLA_Q_RANK, MLA_HEADS, MLA_NOPE_DIM + MLA_ROPE_DIM)
    wuq = jnp.concatenate([uq[:, :, :MLA_NOPE_DIM].reshape(MLA_Q_RANK, -1),
                           uq[:, :, MLA_NOPE_DIM:].reshape(MLA_Q_RANK, -1)], axis=1)
    ukv = mla_w_ukv.astype(BF16).reshape(MLA_KV_RANK, MLA_HEADS, MLA_NOPE_DIM + MLA_V_DIM)
    wukv = jnp.concatenate([ukv[:, :, :MLA_NOPE_DIM].reshape(MLA_KV_RANK, -1),
                            ukv[:, :, MLA_NOPE_DIM:].reshape(MLA_KV_RANK, -1)], axis=1)
    return (wq, wk, wv, wcq, wckv, wkr, wuv, wg, wuq, wukv,
            _group_mean_matrix(GQA_Q_WIDTH, GQA_HEAD_DIM), _group_mean_matrix(GQA_KV_WIDTH, GQA_HEAD_DIM))


def _small_params(qk_norm, mla_q_norm, mla_kv_norm, gmlp_v_norm):
    width = max(GQA_Q_WIDTH, GMLP_WIDTH)
    row = lambda v: jnp.pad(v, (0, width - v.shape[0]))
    rows = [row(jnp.tile(qk_norm[0], GQA_HEADS)), row(jnp.tile(qk_norm[1], GQA_KV_HEADS)),
            row(mla_q_norm), row(mla_kv_norm), row(gmlp_v_norm)]
    rows += [jnp.zeros((width,), F32)] * (8 - len(rows))
    return jnp.stack(rows).astype(F32)


def _stack_rows(rows, B, D):
    full = [jnp.broadcast_to(r, (B, D)) for r in rows]
    full += [jnp.zeros((B, D), F32)] * (8 - len(full))
    return jnp.stack(full, axis=1).astype(F32)


def _heads_to_slabs(parts, B, Lx, H):
    pieces = [p.reshape(B, Lx, H, p.shape[2] // H) for p in parts]
    used = sum(p.shape[3] for p in pieces)
    if used < LANES:
        pieces.append(jnp.zeros((B, Lx, H, LANES - used), pieces[0].dtype))
    return jnp.transpose(jnp.concatenate(pieces, axis=3), (0, 2, 1, 3))


def _gqa_q_slabs(q, B, Lx):
    qh = jnp.transpose(q.reshape(B, Lx, GQA_HEADS, GQA_HEAD_DIM), (0, 2, 1, 3))
    group = GQA_HEADS // GQA_KV_HEADS
    zeros = jnp.zeros_like(qh)
    in_low = (jnp.arange(GQA_HEADS) // group == 0)[None, :, None, None]
    return jnp.concatenate([jnp.where(in_low, qh, zeros), jnp.where(in_low, zeros, qh)], axis=3)


def _gqa_out(o, B, Lx):
    group = GQA_HEADS // GQA_KV_HEADS
    in_low = (jnp.arange(GQA_HEADS) // group == 0)[None, :, None, None]
    oh = jnp.where(in_low, o[..., :GQA_HEAD_DIM], o[..., GQA_HEAD_DIM:])
    return jnp.transpose(oh, (0, 2, 1, 3)).reshape(B, Lx, GQA_Q_WIDTH)


def _mla_out(o, B, Lx):
    odd = (jnp.arange(MLA_HEADS) % 2 == 1)[None, :, None, None]
    oh = jnp.where(odd, o[..., MLA_V_DIM:], o[..., :MLA_V_DIM])
    return jnp.transpose(oh, (0, 2, 1, 3)).reshape(B, Lx, MLA_V_WIDTH)


def _attention_pair(pq, pk_all, B, Lq, tq, tk):
    q, qn, qr = pq
    k_all, v_all, kn_all, kr_all, vm_all = pk_all
    Lk = k_all.shape[1]
    group = GQA_HEADS // GQA_KV_HEADS
    del group
    gqa = _attn_call(_gqa_q_slabs(q, B, Lq), k_all[:, None], v_all[:, None],
                     lambda h: 0, lambda h: 0, tq, tk)
    kr_rep = jnp.broadcast_to(kr_all[:, :, None, :MLA_ROPE_DIM], (B, Lk, MLA_HEADS, MLA_ROPE_DIM))
    k_m = _heads_to_slabs([kn_all, kr_rep.reshape(B, Lk, MLA_QR_WIDTH)], B, Lk, MLA_HEADS)
    q_m = _heads_to_slabs([qn, qr], B, Lq, MLA_HEADS)
    v_m = jnp.transpose(vm_all.reshape(B, Lk, MLA_HEADS // 2, LANES), (0, 2, 1, 3))
    mla = _attn_call(q_m, k_m, v_m, lambda h: h, lambda h: h // 2, tq, tk)
    return _gqa_out(gqa, B, Lq), _mla_out(mla, B, Lq)


def _route(logits, n_experts, rows_per_block):
    T = logits.shape[0]
    P = T * TOP_K
    top_logit, top_idx = lax.top_k(logits, TOP_K)
    top_w = jax.nn.softmax(top_logit, axis=-1)
    e_flat = top_idx.reshape(P)
    onehot = (e_flat[:, None] == jnp.arange(n_experts, dtype=e_flat.dtype)[None, :]).astype(jnp.int32)
    csum = jnp.cumsum(onehot, axis=0)
    rank = jnp.sum(onehot * (csum - 1), axis=1)
    counts = csum[-1]
    padded = (counts + rows_per_block - 1) // rows_per_block * rows_per_block
    pad_end = jnp.cumsum(padded)
    pad_start = pad_end - padded
    dest = pad_start[e_flat] + rank
    n_blocks = -(-(P + n_experts * (rows_per_block - 1)) // rows_per_block)
    tok_flat = jnp.repeat(jnp.arange(T, dtype=jnp.int32), TOP_K)
    row_token = jnp.zeros((n_blocks * rows_per_block,), jnp.int32).at[dest].set(tok_flat)
    blk_start = jnp.arange(n_blocks, dtype=jnp.int32) * rows_per_block
    blk_expert = jnp.minimum(jnp.searchsorted(pad_end, blk_start, side='right'),
                             n_experts - 1).astype(jnp.int32)
    n_used = (pad_end[-1] // rows_per_block).astype(jnp.int32).reshape(1)
    return top_w, dest.reshape(T, TOP_K), row_token, blk_expert, n_used


def _layer(x, xc, mod, mod_c, rope_lat, rope_ctx, p, update_ctx):
    B, L, D = x.shape
    C = xc.shape[1]
    (norms, w_in, qk_norm, mla_q_norm, mla_kv_norm, mla_w_uq, mla_w_ukv, gmlp_v_norm, gmlp_w_s,
     gmlp_b_s, w_branch, w_out, router_w, router_b, ew1, eb1, ew2, eb2) = p
    n_experts = router_w.shape[1]

    weights = _layer_weights(w_in, mla_w_uq, mla_w_ukv, D)
    small = _small_params(qk_norm, mla_q_norm, mla_kv_norm, gmlp_v_norm)
    sh1, sc1, g1, sh2, sc2, g2 = [mod[:, j] for j in range(N_MOD)]
    sh1c, sc1c, g1c, sh2c, sc2c, g2c = [mod_c[j] for j in range(N_MOD)]

    tm_lat = _pick_tile(L, (256, 128))
    tm_ctx = _pick_tile(C, (256, 128))
    lat = _proj_call(x, _stack_rows([norms[0], 1.0 + sc1, sh1], B, D), rope_lat, small, weights, tm_lat)
    cx = _proj_call(xc, _stack_rows([norms[0], 1.0 + sc1c, sh1c], B, D), rope_ctx, small, weights, tm_ctx)
    q, k, v, qn, qr, kn, vm, kr, u, vg, gates = lat
    qc, kc, vc, qnc, qrc, knc, vmc, krc, uc, vgc, gatesc = cx

    cat = lambda a, b: jnp.concatenate([a, b], axis=1)
    keys_all = (cat(k, kc), cat(v, vc), cat(kn, knc), cat(kr, krc), cat(vm, vmc))
    tq = _pick_tile(L, (512, 256, 128))
    tk = _pick_tile(L + C, (1280, 1024, 640, 512, 256, 128))
    gqa_o, mla_o = _attention_pair((q, qn, qr), keys_all, B, L, tq, tk)

    ws = gmlp_w_s.astype(BF16)
    bs = jnp.repeat(gmlp_b_s.T, GMLP_GROUP_DIM, axis=1).astype(F32)
    wb = w_branch.astype(BF16)
    wo = w_out.astype(BF16)
    rw = jnp.pad(router_w, ((0, 0), (0, LANES - n_experts))).astype(F32)
    rb = jnp.pad(router_b, (0, LANES - n_experts)).reshape(1, LANES).astype(F32)

    tm_merge = _pick_tile(L, (256, 128))
    x1, tok, logits = _merge_call(x, _stack_rows([norms[1], g1, norms[2], 1.0 + sc2, sh2], B, D),
                                  gqa_o, mla_o, u, vg, gates, ws, bs, wb, wo, rw, rb, tm_merge)
    tok_all = tok.reshape(B * L, D)
    logits_all = logits.reshape(B * L, LANES)[:, :n_experts]
    if update_ctx:
        ctx_keys = (kc, vc, knc, krc, vmc)
        tc = _pick_tile(C, (256, 128))
        gqa_c, mla_c = _attention_pair((qc, qnc, qrc), ctx_keys, B, C, tc, tc)
        xc1, tokc, logitsc = _merge_call(
            xc, _stack_rows([norms[1], g1c, norms[2], 1.0 + sc2c, sh2c], B, D),
            gqa_c, mla_c, uc, vgc, gatesc, ws, bs, wb, wo, rw, rb, tm_ctx)
        tok_all = jnp.concatenate([tok_all, tokc.reshape(B * C, D)], axis=0)
        logits_all = jnp.concatenate([logits_all, logitsc.reshape(B * C, LANES)[:, :n_experts]], axis=0)

    rows_per_block = 256
    top_w, dest, row_token, blk_expert, n_used = _route(logits_all, n_experts, rows_per_block)
    xb = jnp.take(tok_all, row_token, axis=0)
    y = _expert_call(blk_expert, n_used, xb, ew1.astype(BF16), eb1, ew2.astype(BF16), eb2, rows_per_block)
    f = jnp.sum(jnp.take(y, dest, axis=0) * top_w[:, :, None], axis=1)

    x2 = _final_call(x1, f[:B * L].reshape(B, L, D), _stack_rows([norms[3], g2], B, D), tm_merge)
    if update_ctx:
        xc = _final_call(xc1, f[B * L:].reshape(B, C, D), _stack_rows([norms[3], g2c], B, D), tm_ctx)
    return x2, xc


def kernel(x, c, ctx, c_ctx, w_mod, b_mod, norm_gains, w_in, qk_norm, mla_q_norm, mla_kv_norm, mla_w_uq,
           mla_w_ukv, gmlp_v_norm, gmlp_w_s, gmlp_b_s, w_branch, w_out, router_w, router_b, expert_w1,
           expert_b1, expert_w2, expert_b2):
    B, L, D = x.shape
    C = ctx.shape[1]
    depth = w_mod.shape[0]
    rope_lat = _rope_tables(L)
    rope_ctx = _identity_rope(C)
    s_c = jax.nn.silu(c)
    s_cc = jax.nn.silu(c_ctx)
    xc = ctx
    for i in range(depth):
        mod = (jnp.dot(s_c, w_mod[i], precision=lax.Precision.HIGHEST) + b_mod[i]).reshape(B, N_MOD, D)
        mod_c = (jnp.dot(s_cc, w_mod[i], precision=lax.Precision.HIGHEST) + b_mod[i]).reshape(N_MOD, D)
        params = (norm_gains[i], w_in[i], qk_norm[i], mla_q_norm[i], mla_kv_norm[i], mla_w_uq[i],
                  mla_w_ukv[i], gmlp_v_norm[i], gmlp_w_s[i], gmlp_b_s[i], w_branch[i], w_out[i],
                  router_w[i], router_b[i], expert_w1[i], expert_b1[i], expert_w2[i], expert_b2[i])
        x, xc = _layer(x, xc, mod, mod_c, rope_lat, rope_ctx, params, update_ctx=(i < depth - 1))
    return x
```

```python
import functools
import math

import jax
import jax.numpy as jnp
import numpy as np
from jax import lax
from jax.experimental import pallas as pl
from jax.experimental.pallas import tpu as pltpu

GRID_W = 64
ROPE_THETA = 10000.0
RMS_EPS = 1e-6
N_MOD = 6
GQA_HEADS = 8
GQA_KV_HEADS = 2
GQA_HEAD_DIM = 64
MLA_HEADS = 8
MLA_NOPE_DIM = 64
MLA_ROPE_DIM = 32
MLA_V_DIM = 64
MLA_Q_RANK = 256
MLA_KV_RANK = 128
GMLP_GROUPS = 8
GMLP_GROUP_DIM = 64
GMLP_CHUNK = 128
N_BRANCHES = 3
TOP_K = 4
SWIGLU_ALPHA = 1.702
SWIGLU_LIMIT = 7.0

LANES = 128
VMEM_LIMIT_BYTES = 56 * 2**20
NEG_BIG = -1e30
ATTN_CHUNK = 512
ATTN_UNROLL_MAX = 4
LOG2E = math.log2(math.e)

BF16 = jnp.bfloat16
F32 = jnp.float32

GQA_Q_WIDTH = GQA_HEADS * GQA_HEAD_DIM
GQA_KV_WIDTH = GQA_KV_HEADS * GQA_HEAD_DIM
GMLP_WIDTH = GMLP_GROUPS * GMLP_GROUP_DIM
MLA_QN_WIDTH = MLA_HEADS * MLA_NOPE_DIM
MLA_QR_WIDTH = MLA_HEADS * MLA_ROPE_DIM
MLA_V_WIDTH = MLA_HEADS * MLA_V_DIM


def _pick_tile(n, candidates):
    for t in candidates:
        if n % t == 0:
            return t
    raise ValueError(f"no tile in {candidates} divides {n}")


def _full_spec(a):
    nd = a.ndim
    return pl.BlockSpec(a.shape, lambda *_: (0,) * nd)


def _row_rms(x, gain_row):
    ms = jnp.mean(x * x, axis=-1, keepdims=True)
    return x * lax.rsqrt(ms + RMS_EPS) * gain_row


def _group_rms(x, group_mean_mat, gain_row):
    ms = jnp.dot((x * x).astype(BF16), group_mean_mat, preferred_element_type=F32)
    return x * lax.rsqrt(ms + RMS_EPS) * gain_row


def _rope_lanes(x, cos_t, sin_t, half):
    rows, n = x.shape
    lane = lax.broadcasted_iota(jnp.int32, (rows, LANES), 1)
    first_half = (lane % (2 * half)) < half
    out = []
    for j in range(n // LANES):
        xb = x[:, j * LANES:(j + 1) * LANES]
        partner_up = pltpu.roll(xb, LANES - half, 1)
        partner_dn = pltpu.roll(xb, half, 1)
        partner = jnp.where(first_half, partner_up, partner_dn)
        out.append(xb * cos_t + partner * sin_t)
    return out[0] if len(out) == 1 else jnp.concatenate(out, axis=1)


def _proj_kernel(x_ref, mod_ref, rope_ref, small_ref,
                 wq_ref, wk_ref, wv_ref, wcq_ref, wckv_ref, wkr_ref, wuv_ref, wg_ref,
                 wuq_ref, wukv_ref, gq_ref, gk_ref,
                 q_ref, k_ref, v_ref, qn_ref, qr_ref, kn_ref, vm_ref, kr_ref, u_ref, vg_ref, g_ref):
    x = x_ref[0]
    mod = mod_ref[0]
    h = _row_rms(x, mod[0:1]) * mod[1:2] + mod[2:3]
    hb = h.astype(BF16)

    cos64, sin64 = rope_ref[0], rope_ref[1]
    cos32, sin32 = rope_ref[2], rope_ref[3]
    cos_kr, sin_kr = rope_ref[4], rope_ref[5]

    small = small_ref[...]
    qn_gain = small[0:1, :GQA_Q_WIDTH]
    kn_gain = small[1:2, :GQA_KV_WIDTH]
    cq_gain = small[2:3, :MLA_Q_RANK]
    ckv_gain = small[3:4, :MLA_KV_RANK]
    vg_gain = small[4:5, :GMLP_WIDTH]

    def proj(w_ref):
        return jnp.dot(hb, w_ref[...], preferred_element_type=F32)

    q = _group_rms(proj(wq_ref), gq_ref[...], qn_gain)
    q = _rope_lanes(q, cos64, sin64, GQA_HEAD_DIM // 2) * (GQA_HEAD_DIM ** -0.5 * LOG2E)
    q_ref[0] = q.astype(BF16)
    k = _group_rms(proj(wk_ref), gk_ref[...], kn_gain)
    k_ref[0] = _rope_lanes(k, cos64, sin64, GQA_HEAD_DIM // 2).astype(BF16)
    v_ref[0] = proj(wv_ref).astype(BF16)

    cq = _row_rms(proj(wcq_ref), cq_gain).astype(BF16)
    qm = jnp.dot(cq, wuq_ref[...], preferred_element_type=F32)
    mla_scale = (MLA_NOPE_DIM + MLA_ROPE_DIM) ** -0.5 * LOG2E
    qn_ref[0] = (qm[:, :MLA_QN_WIDTH] * mla_scale).astype(BF16)
    qr = _rope_lanes(qm[:, MLA_QN_WIDTH:], cos32, sin32, MLA_ROPE_DIM // 2) * mla_scale
    qr_ref[0] = qr.astype(BF16)
    ckv = _row_rms(proj(wckv_ref), ckv_gain).astype(BF16)
    kv = jnp.dot(ckv, wukv_ref[...], preferred_element_type=F32)
    kn_ref[0] = kv[:, :MLA_QN_WIDTH].astype(BF16)
    vm_ref[0] = kv[:, MLA_QN_WIDTH:].astype(BF16)
    kr_ref[0] = _rope_lanes(proj(wkr_ref), cos_kr, sin_kr, MLA_ROPE_DIM // 2).astype(BF16)

    act = jax.nn.gelu(proj(wuv_ref), approximate=True)
    u_ref[0] = act[:, :GMLP_WIDTH].astype(BF16)
    vg_ref[0] = _row_rms(act[:, GMLP_WIDTH:], vg_gain).astype(BF16)

    g_ref[0] = proj(wg_ref).astype(BF16)


def _proj_call(x, mod3, rope, small, weights, tm):
    B, L, D = x.shape
    widths = (GQA_Q_WIDTH, GQA_KV_WIDTH, GQA_KV_WIDTH, MLA_QN_WIDTH, MLA_QR_WIDTH, MLA_QN_WIDTH,
              MLA_V_WIDTH, LANES, GMLP_WIDTH, GMLP_WIDTH, N_BRANCHES * D)
    tok_spec = lambda w: pl.BlockSpec((1, tm, w), lambda b, i: (b, i, 0))
    in_specs = [tok_spec(D),
                pl.BlockSpec((1,) + mod3.shape[1:], lambda b, i: (b, 0, 0)),
                pl.BlockSpec((rope.shape[0], tm, LANES), lambda b, i: (0, i, 0)),
                _full_spec(small)] + [_full_spec(w) for w in weights]
    return pl.pallas_call(
        _proj_kernel,
        grid=(B, L // tm),
        in_specs=in_specs,
        out_specs=[tok_spec(w) for w in widths],
        out_shape=[jax.ShapeDtypeStruct((B, L, w), BF16) for w in widths],
        compiler_params=pltpu.CompilerParams(
            dimension_semantics=("parallel", "parallel"), vmem_limit_bytes=VMEM_LIMIT_BYTES),
        name="proj",
    )(x, mod3, rope, small, *weights)


def _attn_kernel(*refs, n_main, has_tail):
    if has_tail:
        q_ref, k_ref, vt_ref, kt_ref, vtt_ref, o_ref, sa_ref, sb_ref, m_ref, l_ref, acc_ref = refs
    else:
        q_ref, k_ref, vt_ref, o_ref, sa_ref, sb_ref, m_ref, l_ref, acc_ref = refs
        kt_ref = vtt_ref = None
    bufs = (sa_ref, sb_ref)
    q = q_ref[0, 0]

    m_ref[...] = jnp.full(m_ref.shape, NEG_BIG, F32)
    l_ref[...] = jnp.zeros(l_ref.shape, F32)
    acc_ref[...] = jnp.zeros(acc_ref.shape, F32)

    def scores(k_chunk, dst_ref):
        dst_ref[0:k_chunk.shape[0], :] = lax.dot_general(
            k_chunk, q, (((1,), (1,)), ((), ())), preferred_element_type=F32)

    def absorb(src_ref, vt_chunk):
        st = src_ref[0:vt_chunk.shape[1], :]
        m = m_ref[...]
        m_new = jnp.maximum(m, jnp.max(st, axis=0, keepdims=True))
        alpha = jnp.exp2(m - m_new)
        pt = jnp.exp2(st - m_new)
        l_ref[...] = alpha * l_ref[...] + jnp.sum(pt, axis=0, keepdims=True)
        acc_ref[...] = alpha * acc_ref[...] + jnp.dot(vt_chunk, pt.astype(BF16),
                                                      preferred_element_type=F32)
        m_ref[...] = m_new

    main_k = lambda c: k_ref[0, 0, c]
    main_vt = lambda c: vt_ref[0, 0, c]

    scores(main_k(0), sa_ref)
    n_pairs = (n_main - 1) // 2 if n_main > ATTN_UNROLL_MAX else 0
    if n_pairs:
        def pair(p, carry):
            c = 2 * p
            scores(main_k(c + 1), sb_ref)
            absorb(sa_ref, main_vt(c))
            scores(main_k(c + 2), sa_ref)
            absorb(sb_ref, main_vt(c + 1))
            return carry
        lax.fori_loop(0, n_pairs, pair, 0)
    rest = [(main_k, main_vt, c) for c in range(2 * n_pairs, n_main)]
    if has_tail:
        rest.append((lambda _: kt_ref[0, 0], lambda _: vtt_ref[0, 0], 0))
    for j, (k_of, vt_of, c) in enumerate(rest):
        if j + 1 < len(rest):
            nk_of, _, nc = rest[j + 1]
            scores(nk_of(nc), bufs[(j + 1) % 2])
        absorb(bufs[j % 2], vt_of(c))

    o_ref[0, 0] = (acc_ref[...] / l_ref[...]).astype(o_ref.dtype)


def _attn_call(q, k, vt, k_tail, vt_tail, k_head_of, v_head_of, tq):
    B, H, Lq, _ = q.shape
    _, _, n_main, kc, _ = k.shape
    dv = vt.shape[3]
    has_tail = k_tail is not None
    in_specs = [pl.BlockSpec((1, 1, tq, LANES), lambda b, h, i: (b, h, i, 0)),
                pl.BlockSpec((1, 1, n_main, kc, LANES), lambda b, h, i: (b, k_head_of(h), 0, 0, 0)),
                pl.BlockSpec((1, 1, n_main, dv, kc), lambda b, h, i: (b, v_head_of(h), 0, 0, 0))]
    args = [q, k, vt]
    if has_tail:
        kt = k_tail.shape[2]
        assert kt <= kc
        in_specs += [pl.BlockSpec((1, 1, kt, LANES), lambda b, h, i: (b, k_head_of(h), 0, 0)),
                     pl.BlockSpec((1, 1, dv, kt), lambda b, h, i: (b, v_head_of(h), 0, 0))]
        args += [k_tail, vt_tail]
    return pl.pallas_call(
        functools.partial(_attn_kernel, n_main=n_main, has_tail=has_tail),
        grid=(B, H, Lq // tq),
        in_specs=in_specs,
        out_specs=pl.BlockSpec((1, 1, dv, tq), lambda b, h, i: (b, h, 0, i)),
        out_shape=jax.ShapeDtypeStruct((B, H, dv, Lq), BF16),
        scratch_shapes=[pltpu.VMEM((kc, tq), F32), pltpu.VMEM((kc, tq), F32),
                        pltpu.VMEM((1, tq), F32), pltpu.VMEM((1, tq), F32),
                        pltpu.VMEM((dv, tq), F32)],
        compiler_params=pltpu.CompilerParams(
            dimension_semantics=("parallel", "parallel", "parallel"),
            vmem_limit_bytes=VMEM_LIMIT_BYTES),
        name="attn",
    )(*args)


def _merge_kernel(x_ref, mod_ref, ga_ref, ma_ref, u_ref, vg_ref, g_ref,
                  ws_ref, bs_ref, wb_ref, wo_ref, rw_ref, rb_ref,
                  xo_ref, tok_ref, lg_ref):
    tm = x_ref.shape[1]
    D = x_ref.shape[2]
    mod = mod_ref[0]

    lane = lax.broadcasted_iota(jnp.int32, (GMLP_CHUNK, LANES), 1)
    low_group = lane < GMLP_GROUP_DIM
    chunks = []
    for c in range(tm // GMLP_CHUNK):
        rows = slice(c * GMLP_CHUNK, (c + 1) * GMLP_CHUNK)
        cols = []
        for j in range(GMLP_WIDTH // LANES):
            vc = vg_ref[0, rows, j * LANES:(j + 1) * LANES]
            s_lo = jnp.dot(ws_ref[2 * j], vc, preferred_element_type=F32)
            s_hi = jnp.dot(ws_ref[2 * j + 1], vc, preferred_element_type=F32)
            cols.append(jnp.where(low_group, s_lo, s_hi))
        s = jnp.concatenate(cols, axis=1) + bs_ref[...]
        chunks.append((u_ref[0, rows, :].astype(F32) * s).astype(BF16))
    gm = chunks[0] if len(chunks) == 1 else jnp.concatenate(chunks, axis=0)

    branches = (ga_ref[0], ma_ref[0], gm)
    merged = None
    for i in range(N_BRANCHES):
        gate = jax.nn.sigmoid(g_ref[0, :, i * D:(i + 1) * D].astype(F32))
        term = gate * jnp.dot(branches[i], wb_ref[i], preferred_element_type=F32)
        merged = term if merged is None else merged + term
    y = jnp.dot(merged.astype(BF16), wo_ref[...], preferred_element_type=F32)
    x_new = x_ref[0] + mod[1:2] * _row_rms(y, mod[0:1])
    xo_ref[0] = x_new

    tok = _row_rms(x_new, mod[2:3]) * mod[3:4] + mod[4:5]
    tok_ref[0] = tok.astype(BF16)
    lg_ref[0] = jnp.dot(tok, rw_ref[...], preferred_element_type=F32,
                        precision=lax.Precision.HIGHEST) + rb_ref[...]


def _merge_call(x, mod5, ga, ma, u, vg, g, ws, bs, wb, wo, rw, rb, tm):
    B, L, D = x.shape
    tok_spec = lambda w: pl.BlockSpec((1, tm, w), lambda b, i: (b, i, 0))
    consts = (ws, bs, wb, wo, rw, rb)
    return pl.pallas_call(
        _merge_kernel,
        grid=(B, L // tm),
        in_specs=[tok_spec(D), pl.BlockSpec((1,) + mod5.shape[1:], lambda b, i: (b, 0, 0)),
                  tok_spec(GQA_Q_WIDTH), tok_spec(MLA_V_WIDTH), tok_spec(GMLP_WIDTH),
                  tok_spec(GMLP_WIDTH), tok_spec(N_BRANCHES * D)] + [_full_spec(a) for a in consts],
        out_specs=[tok_spec(D), tok_spec(D), tok_spec(LANES)],
        out_shape=[jax.ShapeDtypeStruct((B, L, D), F32), jax.ShapeDtypeStruct((B, L, D), BF16),
                   jax.ShapeDtypeStruct((B, L, LANES), F32)],
        compiler_params=pltpu.CompilerParams(
            dimension_semantics=("parallel", "parallel"), vmem_limit_bytes=VMEM_LIMIT_BYTES),
        name="merge",
    )(x, mod5, ga, ma, u, vg, g, *consts)


def _expert_kernel(blk_expert_ref, n_used_ref, xb_ref, w1_ref, b1_ref, w2_ref, b2_ref, y_ref):
    del blk_expert_ref
    d_expert = w2_ref.shape[1]

    @pl.when(pl.program_id(0) < n_used_ref[0])
    def _():
        a = jnp.dot(xb_ref[...], w1_ref[0], preferred_element_type=F32) + b1_ref[0]
        glu = jnp.minimum(a[:, :d_expert], SWIGLU_LIMIT)
        lin = jnp.clip(a[:, d_expert:], -SWIGLU_LIMIT, SWIGLU_LIMIT)
        act = glu * jax.nn.sigmoid(SWIGLU_ALPHA * glu) * (lin + 1.0)
        y_ref[...] = jnp.dot(act.astype(BF16), w2_ref[0], preferred_element_type=F32) + b2_ref[0]

    @pl.when(pl.program_id(0) >= n_used_ref[0])
    def _():
        y_ref[...] = jnp.zeros(y_ref.shape, y_ref.dtype)


def _expert_call(blk_expert, n_used, xb, w1, b1, w2, b2, rows_per_block):
    n_rows, D = xb.shape
    E, _, two_de = w1.shape
    d_expert = w2.shape[1]
    grid_spec = pltpu.PrefetchScalarGridSpec(
        num_scalar_prefetch=2,
        grid=(n_rows // rows_per_block,),
        in_specs=[pl.BlockSpec((rows_per_block, D), lambda i, be, nu: (i, 0)),
                  pl.BlockSpec((1, D, two_de), lambda i, be, nu: (be[i], 0, 0)),
                  pl.BlockSpec((1, 1, two_de), lambda i, be, nu: (be[i], 0, 0)),
                  pl.BlockSpec((1, d_expert, D), lambda i, be, nu: (be[i], 0, 0)),
                  pl.BlockSpec((1, 1, D), lambda i, be, nu: (be[i], 0, 0))],
        out_specs=pl.BlockSpec((rows_per_block, D), lambda i, be, nu: (i, 0)),
    )
    return pl.pallas_call(
        _expert_kernel,
        grid_spec=grid_spec,
        out_shape=jax.ShapeDtypeStruct((n_rows, D), F32),
        compiler_params=pltpu.CompilerParams(
            dimension_semantics=("arbitrary",), vmem_limit_bytes=VMEM_LIMIT_BYTES),
        name="experts",
    )(blk_expert, n_used, xb, w1, b1.reshape(E, 1, two_de), w2, b2.reshape(E, 1, D))


def _final_kernel(x_ref, f_ref, mod_ref, o_ref):
    mod = mod_ref[0]
    o_ref[0] = x_ref[0] + mod[1:2] * _row_rms(f_ref[0], mod[0:1])


def _final_call(x, f, mod2, tm):
    B, L, D = x.shape
    tok_spec = pl.BlockSpec((1, tm, D), lambda b, i: (b, i, 0))
    return pl.pallas_call(
        _final_kernel,
        grid=(B, L // tm),
        in_specs=[tok_spec, tok_spec, pl.BlockSpec((1,) + mod2.shape[1:], lambda b, i: (b, 0, 0))],
        out_specs=tok_spec,
        out_shape=jax.ShapeDtypeStruct((B, L, D), F32),
        compiler_params=pltpu.CompilerParams(
            dimension_semantics=("parallel", "parallel"), vmem_limit_bytes=VMEM_LIMIT_BYTES),
        name="final",
    )(x, f, mod2)


def _rope_tables(n_lat, dtype=F32):
    rows = n_lat // GRID_W
    row = jnp.repeat(jnp.arange(rows, dtype=F32), GRID_W)
    col = jnp.tile(jnp.arange(GRID_W, dtype=F32), rows)

    def tables(rot_dim):
        quarter = rot_dim // 4
        inv_freq = ROPE_THETA ** (-jnp.arange(quarter, dtype=F32) / quarter)
        ang = jnp.concatenate([row[:, None] * inv_freq, col[:, None] * inv_freq], axis=-1)
        cos = jnp.concatenate([jnp.cos(ang), jnp.cos(ang)], axis=-1)
        sin = jnp.concatenate([-jnp.sin(ang), jnp.sin(ang)], axis=-1)
        return cos, sin

    cos64, sin64 = tables(GQA_HEAD_DIM)
    cos32, sin32 = tables(MLA_ROPE_DIM)
    tile = lambda t: jnp.tile(t, (1, LANES // t.shape[1]))
    pad_one = lambda t: jnp.concatenate([t, jnp.ones((n_lat, LANES - t.shape[1]), F32)], axis=1)
    pad_zero = lambda t: jnp.concatenate([t, jnp.zeros((n_lat, LANES - t.shape[1]), F32)], axis=1)
    return jnp.stack([tile(cos64), tile(sin64), tile(cos32), tile(sin32),
                      pad_one(cos32), pad_zero(sin32)]).astype(dtype)


def _identity_rope(n):
    one, zero = jnp.ones((n, LANES), F32), jnp.zeros((n, LANES), F32)
    return jnp.stack([one, zero, one, zero, one, zero])


def _group_mean_matrix(width, group):
    idx = np.arange(width) // group
    return jnp.asarray((idx[:, None] == idx[None, :]).astype(np.float32) / group, dtype=BF16)


def _layer_weights(w_in, mla_w_uq, mla_w_ukv, D):
    splits = np.cumsum((GQA_Q_WIDTH, GQA_KV_WIDTH, GQA_KV_WIDTH, MLA_Q_RANK, MLA_KV_RANK, MLA_ROPE_DIM,
                        2 * GMLP_WIDTH, N_BRANCHES * D))[:-1].tolist()
    wq, wk, wv, wcq, wckv, wkr, wuv, wg = jnp.split(w_in.astype(BF16), splits, axis=1)
    wkr = jnp.pad(wkr, ((0, 0), (0, LANES - MLA_ROPE_DIM)))
    uq = mla_w_uq.astype(BF16).reshape(MLA_Q_RANK, MLA_HEADS, MLA_NOPE_DIM + MLA_ROPE_DIM)
    wuq = jnp.concatenate([uq[:, :, :MLA_NOPE_DIM].reshape(MLA_Q_RANK, -1),
                           uq[:, :, MLA_NOPE_DIM:].reshape(MLA_Q_RANK, -1)], axis=1)
    ukv = mla_w_ukv.astype(BF16).reshape(MLA_KV_RANK, MLA_HEADS, MLA_NOPE_DIM + MLA_V_DIM)
    wukv = jnp.concatenate([ukv[:, :, :MLA_NOPE_DIM].reshape(MLA_KV_RANK, -1),
                            ukv[:, :, MLA_NOPE_DIM:].reshape(MLA_KV_RANK, -1)], axis=1)
    return (wq, wk, wv, wcq, wckv, wkr, wuv, wg, wuq, wukv,
            _group_mean_matrix(GQA_Q_WIDTH, GQA_HEAD_DIM), _group_mean_matrix(GQA_KV_WIDTH, GQA_HEAD_DIM))


def _small_params(qk_norm, mla_q_norm, mla_kv_norm, gmlp_v_norm):
    width = max(GQA_Q_WIDTH, GMLP_WIDTH)
    row = lambda v: jnp.pad(v, (0, width - v.shape[0]))
    rows = [row(jnp.tile(qk_norm[0], GQA_HEADS)), row(jnp.tile(qk_norm[1], GQA_KV_HEADS)),
            row(mla_q_norm), row(mla_kv_norm), row(gmlp_v_norm)]
    rows += [jnp.zeros((width,), F32)] * (8 - len(rows))
    return jnp.stack(rows).astype(F32)


def _stack_rows(rows, B, D):
    full = [jnp.broadcast_to(r, (B, D)) for r in rows]
    full += [jnp.zeros((B, D), F32)] * (8 - len(full))
    return jnp.stack(full, axis=1).astype(F32)


def _heads_to_slabs(parts, B, Lx, H):
    pieces = [p.reshape(B, Lx, H, p.shape[2] // H) for p in parts]
    used = sum(p.shape[3] for p in pieces)
    if used < LANES:
        pieces.append(jnp.zeros((B, Lx, H, LANES - used), pieces[0].dtype))
    return jnp.transpose(jnp.concatenate(pieces, axis=3), (0, 2, 1, 3))


def _gqa_q_slabs(q, B, Lx):
    qh = jnp.transpose(q.reshape(B, Lx, GQA_HEADS, GQA_HEAD_DIM), (0, 2, 1, 3))
    group = GQA_HEADS // GQA_KV_HEADS
    zeros = jnp.zeros_like(qh)
    in_low = (jnp.arange(GQA_HEADS) // group == 0)[None, :, None, None]
    return jnp.concatenate([jnp.where(in_low, qh, zeros), jnp.where(in_low, zeros, qh)], axis=3)


def _heads_last(ot, B, Lx):
    return jnp.transpose(ot, (0, 3, 1, 2)).reshape(B, Lx, ot.shape[1] * ot.shape[2])


def _values_transposed(v, B, Lk, H):
    return jnp.transpose(v, (0, 2, 1)).reshape(B, H, v.shape[2] // H, Lk)


def _chunk_keys(k, vt, kc):
    B, H, Lk, _ = k.shape
    dv = vt.shape[2]
    n = Lk // kc
    return (k.reshape(B, H, n, kc, LANES),
            jnp.transpose(vt.reshape(B, vt.shape[1], dv, n, kc), (0, 1, 3, 2, 4)))


def _mixer_keys(keys, B):
    k, v, kn, kr, vm = keys
    Lk = k.shape[1]
    kr_rep = jnp.broadcast_to(kr[:, :, None, :MLA_ROPE_DIM], (B, Lk, MLA_HEADS, MLA_ROPE_DIM))
    return dict(
        gqa_k=k[:, None], gqa_vt=_values_transposed(v, B, Lk, GQA_KV_HEADS),
        mla_k=_heads_to_slabs([kn, kr_rep.reshape(B, Lk, MLA_QR_WIDTH)], B, Lk, MLA_HEADS),
        mla_vt=_values_transposed(vm, B, Lk, MLA_HEADS))


def _attention_pair(pq, main, tail, B, Lq, tq):
    q, qn, qr = pq
    kc = _pick_tile(main["gqa_k"].shape[2], (ATTN_CHUNK, 256, 128))
    group = GQA_HEADS // GQA_KV_HEADS
    tail_of = lambda name: None if tail is None else tail[name]
    gk, gvt = _chunk_keys(main["gqa_k"], main["gqa_vt"], kc)
    gqa = _attn_call(_gqa_q_slabs(q, B, Lq), gk, gvt, tail_of("gqa_k"), tail_of("gqa_vt"),
                     lambda h: 0, lambda h: h // group, tq)
    mk, mvt = _chunk_keys(main["mla_k"], main["mla_vt"], kc)
    mla = _attn_call(_heads_to_slabs([qn, qr], B, Lq, MLA_HEADS), mk, mvt,
                     tail_of("mla_k"), tail_of("mla_vt"), lambda h: h, lambda h: h, tq)
    return _heads_last(gqa, B, Lq), _heads_last(mla, B, Lq)


def _route(logits, n_experts, rows_per_block):
    T = logits.shape[0]
    P = T * TOP_K
    top_logit, top_idx = lax.top_k(logits, TOP_K)
    top_w = jax.nn.softmax(top_logit, axis=-1)
    e_flat = top_idx.reshape(P)
    onehot = (e_flat[:, None] == jnp.arange(n_experts, dtype=e_flat.dtype)[None, :]).astype(jnp.int32)
    csum = jnp.cumsum(onehot, axis=0)
    rank = jnp.sum(onehot * (csum - 1), axis=1)
    counts = csum[-1]
    padded = (counts + rows_per_block - 1) // rows_per_block * rows_per_block
    pad_end = jnp.cumsum(padded)
    pad_start = pad_end - padded
    dest = pad_start[e_flat] + rank
    n_blocks = -(-(P + n_experts * (rows_per_block - 1)) // rows_per_block)
    tok_flat = jnp.repeat(jnp.arange(T, dtype=jnp.int32), TOP_K)
    row_token = jnp.zeros((n_blocks * rows_per_block,), jnp.int32).at[dest].set(tok_flat)
    blk_start = jnp.arange(n_blocks, dtype=jnp.int32) * rows_per_block
    blk_expert = jnp.minimum(jnp.searchsorted(pad_end, blk_start, side='right'),
                             n_experts - 1).astype(jnp.int32)
    n_used = (pad_end[-1] // rows_per_block).astype(jnp.int32).reshape(1)
    return top_w, dest.reshape(T, TOP_K), row_token, blk_expert, n_used


def _layer(x, xc, mod, mod_c, rope_lat, rope_ctx, p, update_ctx):
    B, L, D = x.shape
    C = xc.shape[1]
    (norms, w_in, qk_norm, mla_q_norm, mla_kv_norm, mla_w_uq, mla_w_ukv, gmlp_v_norm, gmlp_w_s,
     gmlp_b_s, w_branch, w_out, router_w, router_b, ew1, eb1, ew2, eb2) = p
    n_experts = router_w.shape[1]

    weights = _layer_weights(w_in, mla_w_uq, mla_w_ukv, D)
    small = _small_params(qk_norm, mla_q_norm, mla_kv_norm, gmlp_v_norm)
    sh1, sc1, g1, sh2, sc2, g2 = [mod[:, j] for j in range(N_MOD)]
    sh1c, sc1c, g1c, sh2c, sc2c, g2c = [mod_c[j] for j in range(N_MOD)]

    tm_lat = _pick_tile(L, (256, 128))
    tm_ctx = _pick_tile(C, (256, 128))
    lat = _proj_call(x, _stack_rows([norms[0], 1.0 + sc1, sh1], B, D), rope_lat, small, weights, tm_lat)
    cx = _proj_call(xc, _stack_rows([norms[0], 1.0 + sc1c, sh1c], B, D), rope_ctx, small, weights, tm_ctx)
    q, k, v, qn, qr, kn, vm, kr, u, vg, gates = lat
    qc, kc, vc, qnc, qrc, knc, vmc, krc, uc, vgc, gatesc = cx

    lat_keys = _mixer_keys((k, v, kn, kr, vm), B)
    ctx_keys = _mixer_keys((kc, vc, knc, krc, vmc), B)
    tq = _pick_tile(L, (512, 256, 128))
    gqa_o, mla_o = _attention_pair((q, qn, qr), lat_keys, ctx_keys, B, L, tq)

    ws = gmlp_w_s.astype(BF16)
    bs = jnp.repeat(gmlp_b_s.T, GMLP_GROUP_DIM, axis=1).astype(F32)
    wb = w_branch.astype(BF16)
    wo = w_out.astype(BF16)
    rw = jnp.pad(router_w, ((0, 0), (0, LANES - n_experts))).astype(F32)
    rb = jnp.pad(router_b, (0, LANES - n_experts)).reshape(1, LANES).astype(F32)

    tm_merge = _pick_tile(L, (256, 128))
    x1, tok, logits = _merge_call(x, _stack_rows([norms[1], g1, norms[2], 1.0 + sc2, sh2], B, D),
                                  gqa_o, mla_o, u, vg, gates, ws, bs, wb, wo, rw, rb, tm_merge)
    tok_all = tok.reshape(B * L, D)
    logits_all = logits.reshape(B * L, LANES)[:, :n_experts]
    if update_ctx:
        tc = _pick_tile(C, (256, 128))
        gqa_c, mla_c = _attention_pair((qc, qnc, qrc), ctx_keys, None, B, C, tc)
        xc1, tokc, logitsc = _merge_call(
            xc, _stack_rows([norms[1], g1c, norms[2], 1.0 + sc2c, sh2c], B, D),
            gqa_c, mla_c, uc, vgc, gatesc, ws, bs, wb, wo, rw, rb, tm_ctx)
        tok_all = jnp.concatenate([tok_all, tokc.reshape(B * C, D)], axis=0)
        logits_all = jnp.concatenate([logits_all, logitsc.reshape(B * C, LANES)[:, :n_experts]], axis=0)

    rows_per_block = 256
    top_w, dest, row_token, blk_expert, n_used = _route(logits_all, n_experts, rows_per_block)
    xb = jnp.take(tok_all, row_token, axis=0)
    y = _expert_call(blk_expert, n_used, xb, ew1.astype(BF16), eb1, ew2.astype(BF16), eb2, rows_per_block)
    f = jnp.sum(jnp.take(y, dest, axis=0) * top_w[:, :, None], axis=1)

    x2 = _final_call(x1, f[:B * L].reshape(B, L, D), _stack_rows([norms[3], g2], B, D), tm_merge)
    if update_ctx:
        xc = _final_call(xc1, f[B * L:].reshape(B, C, D), _stack_rows([norms[3], g2c], B, D), tm_ctx)
    return x2, xc


def kernel(x, c, ctx, c_ctx, w_mod, b_mod, norm_gains, w_in, qk_norm, mla_q_norm, mla_kv_norm, mla_w_uq,
           mla_w_ukv, gmlp_v_norm, gmlp_w_s, gmlp_b_s, w_branch, w_out, router_w, router_b, expert_w1,
           expert_b1, expert_w2, expert_b2):
    B, L, D = x.shape
    C = ctx.shape[1]
    depth = w_mod.shape[0]
    rope_lat = _rope_tables(L)
    rope_ctx = _identity_rope(C)
    s_c = jax.nn.silu(c)
    s_cc = jax.nn.silu(c_ctx)
    xc = ctx
    for i in range(depth):
        mod = (jnp.dot(s_c, w_mod[i], precision=lax.Precision.HIGHEST) + b_mod[i]).reshape(B, N_MOD, D)
        mod_c = (jnp.dot(s_cc, w_mod[i], precision=lax.Precision.HIGHEST) + b_mod[i]).reshape(N_MOD, D)
        params = (norm_gains[i], w_in[i], qk_norm[i], mla_q_norm[i], mla_kv_norm[i], mla_w_uq[i],
                  mla_w_ukv[i], gmlp_v_norm[i], gmlp_w_s[i], gmlp_b_s[i], w_branch[i], w_out[i],
                  router_w[i], router_b[i], expert_w1[i], expert_b1[i], expert_w2[i], expert_b2[i])
        x, xc = _layer(x, xc, mod, mod_c, rope_lat, rope_ctx, params, update_ctx=(i < depth - 1))
    return x
```

```python
import functools
import math

import jax
import jax.numpy as jnp
import numpy as np
from jax import lax
from jax.experimental import pallas as pl
from jax.experimental.pallas import tpu as pltpu

GRID_W = 64
ROPE_THETA = 10000.0
RMS_EPS = 1e-6
N_MOD = 6
GQA_HEADS = 8
GQA_KV_HEADS = 2
GQA_HEAD_DIM = 64
MLA_HEADS = 8
MLA_NOPE_DIM = 64
MLA_ROPE_DIM = 32
MLA_V_DIM = 64
MLA_Q_RANK = 256
MLA_KV_RANK = 128
GMLP_GROUPS = 8
GMLP_GROUP_DIM = 64
GMLP_CHUNK = 128
N_BRANCHES = 3
TOP_K = 4
SWIGLU_ALPHA = 1.702
SWIGLU_LIMIT = 7.0

LANES = 128
VMEM_LIMIT_BYTES = 56 * 2**20
NEG_BIG = -1e30
ATTN_CHUNK = 1024
ATTN_LOOP_CHUNKS = 4
VT_EXTRA_ROWS = 16
LOG2E = math.log2(math.e)

BF16 = jnp.bfloat16
F32 = jnp.float32

GQA_Q_WIDTH = GQA_HEADS * GQA_HEAD_DIM
GQA_KV_WIDTH = GQA_KV_HEADS * GQA_HEAD_DIM
GMLP_WIDTH = GMLP_GROUPS * GMLP_GROUP_DIM
MLA_QN_WIDTH = MLA_HEADS * MLA_NOPE_DIM
MLA_QR_WIDTH = MLA_HEADS * MLA_ROPE_DIM
MLA_V_WIDTH = MLA_HEADS * MLA_V_DIM


def _pick_tile(n, candidates):
    for t in candidates:
        if n % t == 0:
            return t
    raise ValueError(f"no tile in {candidates} divides {n}")


def _full_spec(a):
    nd = a.ndim
    return pl.BlockSpec(a.shape, lambda *_: (0,) * nd)


def _row_rms(x, gain_row):
    ms = jnp.mean(x * x, axis=-1, keepdims=True)
    return x * lax.rsqrt(ms + RMS_EPS) * gain_row


def _group_rms(x, group_mean_mat, gain_row):
    ms = jnp.dot((x * x).astype(BF16), group_mean_mat, preferred_element_type=F32)
    return x * lax.rsqrt(ms + RMS_EPS) * gain_row


def _rope_lanes(x, cos_t, sin_t, half):
    rows, n = x.shape
    lane = lax.broadcasted_iota(jnp.int32, (rows, LANES), 1)
    first_half = (lane % (2 * half)) < half
    out = []
    for j in range(n // LANES):
        xb = x[:, j * LANES:(j + 1) * LANES]
        partner_up = pltpu.roll(xb, LANES - half, 1)
        partner_dn = pltpu.roll(xb, half, 1)
        partner = jnp.where(first_half, partner_up, partner_dn)
        out.append(xb * cos_t + partner * sin_t)
    return out[0] if len(out) == 1 else jnp.concatenate(out, axis=1)


def _proj_kernel(x_ref, mod_ref, rope_ref, small_ref,
                 wq_ref, wk_ref, wv_ref, wcq_ref, wckv_ref, wkr_ref, wuv_ref, wg_ref,
                 wuq_ref, wukv_ref, gq_ref, gk_ref,
                 q_ref, k_ref, v_ref, qn_ref, qr_ref, kn_ref, vm_ref, kr_ref, u_ref, vg_ref, g_ref):
    x = x_ref[0]
    mod = mod_ref[0]
    h = _row_rms(x, mod[0:1]) * mod[1:2] + mod[2:3]
    hb = h.astype(BF16)

    cos64, sin64 = rope_ref[0], rope_ref[1]
    cos32, sin32 = rope_ref[2], rope_ref[3]
    cos_kr, sin_kr = rope_ref[4], rope_ref[5]

    small = small_ref[...]
    qn_gain = small[0:1, :GQA_Q_WIDTH]
    kn_gain = small[1:2, :GQA_KV_WIDTH]
    cq_gain = small[2:3, :MLA_Q_RANK]
    ckv_gain = small[3:4, :MLA_KV_RANK]
    vg_gain = small[4:5, :GMLP_WIDTH]

    def proj(w_ref):
        return jnp.dot(hb, w_ref[...], preferred_element_type=F32)

    q = _group_rms(proj(wq_ref), gq_ref[...], qn_gain)
    q = _rope_lanes(q, cos64, sin64, GQA_HEAD_DIM // 2) * (GQA_HEAD_DIM ** -0.5 * LOG2E)
    q_ref[0] = q.astype(BF16)
    k = _group_rms(proj(wk_ref), gk_ref[...], kn_gain)
    k_ref[0] = _rope_lanes(k, cos64, sin64, GQA_HEAD_DIM // 2).astype(BF16)
    v_ref[0] = proj(wv_ref).astype(BF16)

    cq = _row_rms(proj(wcq_ref), cq_gain).astype(BF16)
    qm = jnp.dot(cq, wuq_ref[...], preferred_element_type=F32)
    mla_scale = (MLA_NOPE_DIM + MLA_ROPE_DIM) ** -0.5 * LOG2E
    qn_ref[0] = (qm[:, :MLA_QN_WIDTH] * mla_scale).astype(BF16)
    qr = _rope_lanes(qm[:, MLA_QN_WIDTH:], cos32, sin32, MLA_ROPE_DIM // 2) * mla_scale
    qr_ref[0] = qr.astype(BF16)
    ckv = _row_rms(proj(wckv_ref), ckv_gain).astype(BF16)
    kv = jnp.dot(ckv, wukv_ref[...], preferred_element_type=F32)
    kn_ref[0] = kv[:, :MLA_QN_WIDTH].astype(BF16)
    vm_ref[0] = kv[:, MLA_QN_WIDTH:].astype(BF16)
    kr_ref[0] = _rope_lanes(proj(wkr_ref), cos_kr, sin_kr, MLA_ROPE_DIM // 2).astype(BF16)

    act = jax.nn.gelu(proj(wuv_ref), approximate=True)
    u_ref[0] = act[:, :GMLP_WIDTH].astype(BF16)
    vg_ref[0] = _row_rms(act[:, GMLP_WIDTH:], vg_gain).astype(BF16)

    g_ref[0] = proj(wg_ref).astype(BF16)


def _proj_call(x, mod3, rope, small, weights, tm):
    B, L, D = x.shape
    widths = (GQA_Q_WIDTH, GQA_KV_WIDTH, GQA_KV_WIDTH, MLA_QN_WIDTH, MLA_QR_WIDTH, MLA_QN_WIDTH,
              MLA_V_WIDTH, LANES, GMLP_WIDTH, GMLP_WIDTH, N_BRANCHES * D)
    tok_spec = lambda w: pl.BlockSpec((1, tm, w), lambda b, i: (b, i, 0))
    in_specs = [tok_spec(D),
                pl.BlockSpec((1,) + mod3.shape[1:], lambda b, i: (b, 0, 0)),
                pl.BlockSpec((rope.shape[0], tm, LANES), lambda b, i: (0, i, 0)),
                _full_spec(small)] + [_full_spec(w) for w in weights]
    return pl.pallas_call(
        _proj_kernel,
        grid=(B, L // tm),
        in_specs=in_specs,
        out_specs=[tok_spec(w) for w in widths],
        out_shape=[jax.ShapeDtypeStruct((B, L, w), BF16) for w in widths],
        compiler_params=pltpu.CompilerParams(
            dimension_semantics=("parallel", "parallel"), vmem_limit_bytes=VMEM_LIMIT_BYTES),
        name="proj",
    )(x, mod3, rope, small, *weights)


def _attn_kernel(*refs, n_main, has_tail):
    if has_tail:
        q_ref, k_ref, vt_ref, kt_ref, vtt_ref, o_ref, sa_ref, sb_ref, m_ref, acc_ref = refs
    else:
        q_ref, k_ref, vt_ref, o_ref, sa_ref, sb_ref, m_ref, acc_ref = refs
        kt_ref = vtt_ref = None
    bufs = (sa_ref, sb_ref)
    dv = o_ref.shape[2]
    q = q_ref[0, 0]

    m_ref[...] = jnp.full(m_ref.shape, NEG_BIG, F32)
    acc_ref[...] = jnp.zeros(acc_ref.shape, F32)

    def scores(k_chunk, dst_ref):
        dst_ref[0:k_chunk.shape[0], :] = lax.dot_general(
            k_chunk, q, (((1,), (1,)), ((), ())), preferred_element_type=F32)

    def absorb(src_ref, vt_chunk):
        st = src_ref[0:vt_chunk.shape[1], :]
        m = m_ref[...]
        m_new = jnp.maximum(m, jnp.max(st, axis=0, keepdims=True))
        alpha = jnp.exp2(m - m_new)
        pt = jnp.exp2(st - m_new).astype(BF16)
        acc_ref[...] = alpha * acc_ref[...] + jnp.dot(vt_chunk, pt, preferred_element_type=F32)
        m_ref[...] = m_new

    main_k = lambda c: k_ref[0, 0, c]
    main_vt = lambda c: vt_ref[0, 0, c]

    scores(main_k(0), sa_ref)
    group = ATTN_LOOP_CHUNKS
    n_groups = (n_main - 1) // group
    if n_groups:
        def run_group(g, carry):
            for j in range(group):
                c = group * g + j
                scores(main_k(c + 1), bufs[(j + 1) % 2])
                absorb(bufs[j % 2], main_vt(c))
            return carry
        lax.fori_loop(0, n_groups, run_group, 0)
    rest = [(main_k, main_vt, c) for c in range(group * n_groups, n_main)]
    if has_tail:
        rest.append((lambda _: kt_ref[0, 0], lambda _: vtt_ref[0, 0], 0))
    for j, (k_of, vt_of, c) in enumerate(rest):
        if j + 1 < len(rest):
            nk_of, _, nc = rest[j + 1]
            scores(nk_of(nc), bufs[(j + 1) % 2])
        absorb(bufs[j % 2], vt_of(c))

    acc = acc_ref[...]
    o_ref[0, 0] = (acc[0:dv] / acc[dv:dv + 1]).astype(o_ref.dtype)


def _attn_call(q, k, vt, k_tail, vt_tail, k_head_of, v_head_of, tq):
    B, H, Lq, _ = q.shape
    _, _, n_main, kc, _ = k.shape
    dvx = vt.shape[3]
    dv = dvx - VT_EXTRA_ROWS
    has_tail = k_tail is not None
    in_specs = [pl.BlockSpec((1, 1, tq, LANES), lambda b, h, i: (b, h, i, 0)),
                pl.BlockSpec((1, 1, n_main, kc, LANES), lambda b, h, i: (b, k_head_of(h), 0, 0, 0)),
                pl.BlockSpec((1, 1, n_main, dvx, kc), lambda b, h, i: (b, v_head_of(h), 0, 0, 0))]
    args = [q, k, vt]
    if has_tail:
        kt = k_tail.shape[2]
        assert kt <= kc
        in_specs += [pl.BlockSpec((1, 1, kt, LANES), lambda b, h, i: (b, k_head_of(h), 0, 0)),
                     pl.BlockSpec((1, 1, dvx, kt), lambda b, h, i: (b, v_head_of(h), 0, 0))]
        args += [k_tail, vt_tail]
    return pl.pallas_call(
        functools.partial(_attn_kernel, n_main=n_main, has_tail=has_tail),
        grid=(B, H, Lq // tq),
        in_specs=in_specs,
        out_specs=pl.BlockSpec((1, 1, dv, tq), lambda b, h, i: (b, h, 0, i)),
        out_shape=jax.ShapeDtypeStruct((B, H, dv, Lq), BF16),
        scratch_shapes=[pltpu.VMEM((kc, tq), F32), pltpu.VMEM((kc, tq), F32),
                        pltpu.VMEM((1, tq), F32), pltpu.VMEM((dvx, tq), F32)],
        compiler_params=pltpu.CompilerParams(
            dimension_semantics=("parallel", "parallel", "parallel"),
            vmem_limit_bytes=VMEM_LIMIT_BYTES),
        name="attn",
    )(*args)


def _merge_kernel(x_ref, mod_ref, ga_ref, ma_ref, u_ref, vg_ref, g_ref,
                  ws_ref, bs_ref, wb_ref, wo_ref, rw_ref, rb_ref,
                  xo_ref, tok_ref, lg_ref):
    tm = x_ref.shape[1]
    D = x_ref.shape[2]
    mod = mod_ref[0]

    lane = lax.broadcasted_iota(jnp.int32, (GMLP_CHUNK, LANES), 1)
    low_group = lane < GMLP_GROUP_DIM
    chunks = []
    for c in range(tm // GMLP_CHUNK):
        rows = slice(c * GMLP_CHUNK, (c + 1) * GMLP_CHUNK)
        cols = []
        for j in range(GMLP_WIDTH // LANES):
            vc = vg_ref[0, rows, j * LANES:(j + 1) * LANES]
            s_lo = jnp.dot(ws_ref[2 * j], vc, preferred_element_type=F32)
            s_hi = jnp.dot(ws_ref[2 * j + 1], vc, preferred_element_type=F32)
            cols.append(jnp.where(low_group, s_lo, s_hi))
        s = jnp.concatenate(cols, axis=1) + bs_ref[...]
        chunks.append((u_ref[0, rows, :].astype(F32) * s).astype(BF16))
    gm = chunks[0] if len(chunks) == 1 else jnp.concatenate(chunks, axis=0)

    branches = (ga_ref[0], ma_ref[0], gm)
    merged = None
    for i in range(N_BRANCHES):
        gate = jax.nn.sigmoid(g_ref[0, :, i * D:(i + 1) * D].astype(F32))
        term = gate * jnp.dot(branches[i], wb_ref[i], preferred_element_type=F32)
        merged = term if merged is None else merged + term
    y = jnp.dot(merged.astype(BF16), wo_ref[...], preferred_element_type=F32)
    x_new = x_ref[0] + mod[1:2] * _row_rms(y, mod[0:1])
    xo_ref[0] = x_new

    tok = _row_rms(x_new, mod[2:3]) * mod[3:4] + mod[4:5]
    tok_ref[0] = tok.astype(BF16)
    lg_ref[0] = jnp.dot(tok, rw_ref[...], preferred_element_type=F32,
                        precision=lax.Precision.HIGHEST) + rb_ref[...]


def _merge_call(x, mod5, ga, ma, u, vg, g, ws, bs, wb, wo, rw, rb, tm):
    B, L, D = x.shape
    tok_spec = lambda w: pl.BlockSpec((1, tm, w), lambda b, i: (b, i, 0))
    consts = (ws, bs, wb, wo, rw, rb)
    return pl.pallas_call(
        _merge_kernel,
        grid=(B, L // tm),
        in_specs=[tok_spec(D), pl.BlockSpec((1,) + mod5.shape[1:], lambda b, i: (b, 0, 0)),
                  tok_spec(GQA_Q_WIDTH), tok_spec(MLA_V_WIDTH), tok_spec(GMLP_WIDTH),
                  tok_spec(GMLP_WIDTH), tok_spec(N_BRANCHES * D)] + [_full_spec(a) for a in consts],
        out_specs=[tok_spec(D), tok_spec(D), tok_spec(LANES)],
        out_shape=[jax.ShapeDtypeStruct((B, L, D), F32), jax.ShapeDtypeStruct((B, L, D), BF16),
                   jax.ShapeDtypeStruct((B, L, LANES), F32)],
        compiler_params=pltpu.CompilerParams(
            dimension_semantics=("parallel", "parallel"), vmem_limit_bytes=VMEM_LIMIT_BYTES),
        name="merge",
    )(x, mod5, ga, ma, u, vg, g, *consts)


def _expert_kernel(blk_expert_ref, n_used_ref, xb_ref, w1_ref, b1_ref, w2_ref, b2_ref, y_ref,
                   w1b_ref, w2b_ref):
    i = pl.program_id(0)
    d_expert = w2_ref.shape[1]
    new_expert = jnp.logical_or(i == 0, blk_expert_ref[i] != blk_expert_ref[jnp.maximum(i - 1, 0)])

    @pl.when(new_expert)
    def _():
        w1b_ref[...] = w1_ref[0].astype(BF16)
        w2b_ref[...] = w2_ref[0].astype(BF16)

    @pl.when(i < n_used_ref[0])
    def _():
        a = jnp.dot(xb_ref[...], w1b_ref[...], preferred_element_type=F32) + b1_ref[0]
        glu = jnp.minimum(a[:, :d_expert], SWIGLU_LIMIT)
        lin = jnp.clip(a[:, d_expert:], -SWIGLU_LIMIT, SWIGLU_LIMIT)
        act = glu * jax.nn.sigmoid(SWIGLU_ALPHA * glu) * (lin + 1.0)
        y_ref[...] = jnp.dot(act.astype(BF16), w2b_ref[...], preferred_element_type=F32) + b2_ref[0]

    @pl.when(i >= n_used_ref[0])
    def _():
        y_ref[...] = jnp.zeros(y_ref.shape, y_ref.dtype)


def _expert_call(blk_expert, n_used, xb, w1, b1, w2, b2, rows_per_block):
    n_rows, D = xb.shape
    E, _, two_de = w1.shape
    d_expert = w2.shape[1]
    grid_spec = pltpu.PrefetchScalarGridSpec(
        num_scalar_prefetch=2,
        grid=(n_rows // rows_per_block,),
        in_specs=[pl.BlockSpec((rows_per_block, D), lambda i, be, nu: (i, 0)),
                  pl.BlockSpec((1, D, two_de), lambda i, be, nu: (be[i], 0, 0)),
                  pl.BlockSpec((1, 1, two_de), lambda i, be, nu: (be[i], 0, 0)),
                  pl.BlockSpec((1, d_expert, D), lambda i, be, nu: (be[i], 0, 0)),
                  pl.BlockSpec((1, 1, D), lambda i, be, nu: (be[i], 0, 0))],
        out_specs=pl.BlockSpec((rows_per_block, D), lambda i, be, nu: (i, 0)),
        scratch_shapes=[pltpu.VMEM((D, two_de), BF16), pltpu.VMEM((d_expert, D), BF16)],
    )
    return pl.pallas_call(
        _expert_kernel,
        grid_spec=grid_spec,
        out_shape=jax.ShapeDtypeStruct((n_rows, D), F32),
        compiler_params=pltpu.CompilerParams(
            dimension_semantics=("arbitrary",), vmem_limit_bytes=VMEM_LIMIT_BYTES),
        name="experts",
    )(blk_expert, n_used, xb, w1, b1.reshape(E, 1, two_de), w2, b2.reshape(E, 1, D))


def _combine_kernel(dcur_ref, dnxt_ref, y_hbm, w_ref, x_ref, mod_ref, o_ref, gbuf, sem, *, n_tiles):
    s = pl.program_id(0)
    last = n_tiles - 1
    tm = x_ref.shape[0]
    slot = s % 2

    def row_copy(row, k, r, slot_):
        return pltpu.make_async_copy(y_hbm.at[pl.ds(row, 1)], gbuf.at[slot_, k, pl.ds(r, 1)],
                                     sem.at[slot_])

    def start_gather(d_ref, slot_):
        def token(r, carry):
            for k in range(TOP_K):
                row_copy(d_ref[0, 0, r * TOP_K + k], k, r, slot_).start()
            return carry
        lax.fori_loop(0, tm, token, 0)

    @pl.when(s == 0)
    def _():
        start_gather(dcur_ref, 0)

    @pl.when(s < last)
    def _():
        start_gather(dnxt_ref, 1 - slot)

    for k in range(TOP_K):
        pltpu.make_async_copy(y_hbm.at[pl.ds(0, tm)], gbuf.at[slot, k], sem.at[slot]).wait()

    w = w_ref[...]
    f = w[:, 0:1] * gbuf[slot, 0]
    for k in range(1, TOP_K):
        f = f + w[:, k:k + 1] * gbuf[slot, k]
    mod = mod_ref[0]
    o_ref[...] = x_ref[...] + mod[1:2] * _row_rms(f, mod[0:1])


def _combine_call(x, y, dest, top_w, mod2, tm):
    B, L, D = x.shape
    n_tiles = B * L // tm
    tiles_per_mod = n_tiles // mod2.shape[0]
    dest3 = dest.reshape(n_tiles, 1, tm * TOP_K)
    tok_spec = pl.BlockSpec((tm, D), lambda s: (s, 0))
    smem_spec = lambda idx: pl.BlockSpec((1, 1, tm * TOP_K), idx, memory_space=pltpu.SMEM)
    out = pl.pallas_call(
        functools.partial(_combine_kernel, n_tiles=n_tiles),
        grid=(n_tiles,),
        in_specs=[smem_spec(lambda s: (s, 0, 0)),
                  smem_spec(lambda s: (jnp.minimum(s + 1, n_tiles - 1), 0, 0)),
                  pl.BlockSpec(memory_space=pl.ANY),
                  pl.BlockSpec((tm, TOP_K), lambda s: (s, 0)),
                  tok_spec,
                  pl.BlockSpec((1,) + mod2.shape[1:], lambda s: (s // tiles_per_mod, 0, 0))],
        out_specs=tok_spec,
        out_shape=jax.ShapeDtypeStruct((B * L, D), F32),
        scratch_shapes=[pltpu.VMEM((2, TOP_K, tm, D), y.dtype), pltpu.SemaphoreType.DMA((2,))],
        compiler_params=pltpu.CompilerParams(
            dimension_semantics=("arbitrary",), vmem_limit_bytes=VMEM_LIMIT_BYTES),
        name="combine",
    )(dest3, dest3, y, top_w, x.reshape(B * L, D), mod2)
    return out.reshape(B, L, D)


def _rope_tables(n_lat, dtype=F32):
    rows = n_lat // GRID_W
    row = jnp.repeat(jnp.arange(rows, dtype=F32), GRID_W)
    col = jnp.tile(jnp.arange(GRID_W, dtype=F32), rows)

    def tables(rot_dim):
        quarter = rot_dim // 4
        inv_freq = ROPE_THETA ** (-jnp.arange(quarter, dtype=F32) / quarter)
        ang = jnp.concatenate([row[:, None] * inv_freq, col[:, None] * inv_freq], axis=-1)
        cos = jnp.concatenate([jnp.cos(ang), jnp.cos(ang)], axis=-1)
        sin = jnp.concatenate([-jnp.sin(ang), jnp.sin(ang)], axis=-1)
        return cos, sin

    cos64, sin64 = tables(GQA_HEAD_DIM)
    cos32, sin32 = tables(MLA_ROPE_DIM)
    tile = lambda t: jnp.tile(t, (1, LANES // t.shape[1]))
    pad_one = lambda t: jnp.concatenate([t, jnp.ones((n_lat, LANES - t.shape[1]), F32)], axis=1)
    pad_zero = lambda t: jnp.concatenate([t, jnp.zeros((n_lat, LANES - t.shape[1]), F32)], axis=1)
    return jnp.stack([tile(cos64), tile(sin64), tile(cos32), tile(sin32),
                      pad_one(cos32), pad_zero(sin32)]).astype(dtype)


def _identity_rope(n):
    one, zero = jnp.ones((n, LANES), F32), jnp.zeros((n, LANES), F32)
    return jnp.stack([one, zero, one, zero, one, zero])


def _group_mean_matrix(width, group):
    idx = np.arange(width) // group
    return jnp.asarray((idx[:, None] == idx[None, :]).astype(np.float32) / group, dtype=BF16)


def _layer_weights(w_in, mla_w_uq, mla_w_ukv, D):
    splits = np.cumsum((GQA_Q_WIDTH, GQA_KV_WIDTH, GQA_KV_WIDTH, MLA_Q_RANK, MLA_KV_RANK, MLA_ROPE_DIM,
                        2 * GMLP_WIDTH, N_BRANCHES * D))[:-1].tolist()
    wq, wk, wv, wcq, wckv, wkr, wuv, wg = jnp.split(w_in.astype(BF16), splits, axis=1)
    wkr = jnp.pad(wkr, ((0, 0), (0, LANES - MLA_ROPE_DIM)))
    uq = mla_w_uq.astype(BF16).reshape(MLA_Q_RANK, MLA_HEADS, MLA_NOPE_DIM + MLA_ROPE_DIM)
    wuq = jnp.concatenate([uq[:, :, :MLA_NOPE_DIM].reshape(MLA_Q_RANK, -1),
                           uq[:, :, MLA_NOPE_DIM:].reshape(MLA_Q_RANK, -1)], axis=1)
    ukv = mla_w_ukv.astype(BF16).reshape(MLA_KV_RANK, MLA_HEADS, MLA_NOPE_DIM + MLA_V_DIM)
    wukv = jnp.concatenate([ukv[:, :, :MLA_NOPE_DIM].reshape(MLA_KV_RANK, -1),
                            ukv[:, :, MLA_NOPE_DIM:].reshape(MLA_KV_RANK, -1)], axis=1)
    return (wq, wk, wv, wcq, wckv, wkr, wuv, wg, wuq, wukv,
            _group_mean_matrix(GQA_Q_WIDTH, GQA_HEAD_DIM), _group_mean_matrix(GQA_KV_WIDTH, GQA_HEAD_DIM))


def _small_params(qk_norm, mla_q_norm, mla_kv_norm, gmlp_v_norm):
    width = max(GQA_Q_WIDTH, GMLP_WIDTH)
    row = lambda v: jnp.pad(v, (0, width - v.shape[0]))
    rows = [row(jnp.tile(qk_norm[0], GQA_HEADS)), row(jnp.tile(qk_norm[1], GQA_KV_HEADS)),
            row(mla_q_norm), row(mla_kv_norm), row(gmlp_v_norm)]
    rows += [jnp.zeros((width,), F32)] * (8 - len(rows))
    return jnp.stack(rows).astype(F32)


def _stack_rows(rows, B, D):
    full = [jnp.broadcast_to(r, (B, D)) for r in rows]
    full += [jnp.zeros((B, D), F32)] * (8 - len(full))
    return jnp.stack(full, axis=1).astype(F32)


def _heads_to_slabs(parts, B, Lx, H):
    pieces = [p.reshape(B, Lx, H, p.shape[2] // H) for p in parts]
    used = sum(p.shape[3] for p in pieces)
    if used < LANES:
        pieces.append(jnp.zeros((B, Lx, H, LANES - used), pieces[0].dtype))
    return jnp.transpose(jnp.concatenate(pieces, axis=3), (0, 2, 1, 3))


def _gqa_q_slabs(q, B, Lx):
    qh = jnp.transpose(q.reshape(B, Lx, GQA_HEADS, GQA_HEAD_DIM), (0, 2, 1, 3))
    group = GQA_HEADS // GQA_KV_HEADS
    zeros = jnp.zeros_like(qh)
    in_low = (jnp.arange(GQA_HEADS) // group == 0)[None, :, None, None]
    return jnp.concatenate([jnp.where(in_low, qh, zeros), jnp.where(in_low, zeros, qh)], axis=3)


def _heads_last(ot, B, Lx):
    return jnp.transpose(ot, (0, 3, 1, 2)).reshape(B, Lx, ot.shape[1] * ot.shape[2])


def _values_transposed(v, B, Lk, H):
    vt = jnp.transpose(v, (0, 2, 1)).reshape(B, H, v.shape[2] // H, Lk)
    extra = jnp.zeros((B, H, VT_EXTRA_ROWS, Lk), vt.dtype).at[:, :, 0].set(1)
    return jnp.concatenate([vt, extra], axis=2)


def _chunk_keys(k, vt, kc):
    B, H, Lk, _ = k.shape
    dv = vt.shape[2]
    n = Lk // kc
    return (k.reshape(B, H, n, kc, LANES),
            jnp.transpose(vt.reshape(B, vt.shape[1], dv, n, kc), (0, 1, 3, 2, 4)))


def _mixer_keys(keys, B):
    k, v, kn, kr, vm = keys
    Lk = k.shape[1]
    kr_rep = jnp.broadcast_to(kr[:, :, None, :MLA_ROPE_DIM], (B, Lk, MLA_HEADS, MLA_ROPE_DIM))
    return dict(
        gqa_k=k[:, None], gqa_vt=_values_transposed(v, B, Lk, GQA_KV_HEADS),
        mla_k=_heads_to_slabs([kn, kr_rep.reshape(B, Lk, MLA_QR_WIDTH)], B, Lk, MLA_HEADS),
        mla_vt=_values_transposed(vm, B, Lk, MLA_HEADS))


def _attention_pair(pq, main, tail, B, Lq, tq):
    q, qn, qr = pq
    kc = _pick_tile(main["gqa_k"].shape[2], (ATTN_CHUNK, 512, 256, 128))
    group = GQA_HEADS // GQA_KV_HEADS
    tail_of = lambda name: None if tail is None else tail[name]
    gk, gvt = _chunk_keys(main["gqa_k"], main["gqa_vt"], kc)
    gqa = _attn_call(_gqa_q_slabs(q, B, Lq), gk, gvt, tail_of("gqa_k"), tail_of("gqa_vt"),
                     lambda h: 0, lambda h: h // group, tq)
    mk, mvt = _chunk_keys(main["mla_k"], main["mla_vt"], kc)
    mla = _attn_call(_heads_to_slabs([qn, qr], B, Lq, MLA_HEADS), mk, mvt,
                     tail_of("mla_k"), tail_of("mla_vt"), lambda h: h, lambda h: h, tq)
    return _heads_last(gqa, B, Lq), _heads_last(mla, B, Lq)


def _route(logits, n_experts, rows_per_block):
    T = logits.shape[0]
    P = T * TOP_K
    top_logit, top_idx = lax.top_k(logits, TOP_K)
    top_w = jax.nn.softmax(top_logit, axis=-1)
    e_flat = top_idx.reshape(P)
    onehot = (e_flat[:, None] == jnp.arange(n_experts, dtype=e_flat.dtype)[None, :]).astype(jnp.int32)
    csum = jnp.cumsum(onehot, axis=0)
    rank = jnp.sum(onehot * (csum - 1), axis=1)
    counts = csum[-1]
    padded = (counts + rows_per_block - 1) // rows_per_block * rows_per_block
    pad_end = jnp.cumsum(padded)
    pad_start = pad_end - padded
    dest = pad_start[e_flat] + rank
    n_blocks = -(-(P + n_experts * (rows_per_block - 1)) // rows_per_block)
    tok_flat = jnp.repeat(jnp.arange(T, dtype=jnp.int32), TOP_K)
    row_token = jnp.zeros((n_blocks * rows_per_block,), jnp.int32).at[dest].set(tok_flat)
    blk_start = jnp.arange(n_blocks, dtype=jnp.int32) * rows_per_block
    blk_expert = jnp.minimum(jnp.sum((pad_end[None, :] <= blk_start[:, None]).astype(jnp.int32), axis=1),
                             n_experts - 1)
    n_used = (pad_end[-1] // rows_per_block).astype(jnp.int32).reshape(1)
    return top_w, dest.reshape(T, TOP_K), row_token, blk_expert, n_used


def _layer(x, xc, mod, mod_c, rope_lat, rope_ctx, p, update_ctx):
    B, L, D = x.shape
    C = xc.shape[1]
    (norms, w_in, qk_norm, mla_q_norm, mla_kv_norm, mla_w_uq, mla_w_ukv, gmlp_v_norm, gmlp_w_s,
     gmlp_b_s, w_branch, w_out, router_w, router_b, ew1, eb1, ew2, eb2) = p
    n_experts = router_w.shape[1]

    weights = _layer_weights(w_in, mla_w_uq, mla_w_ukv, D)
    small = _small_params(qk_norm, mla_q_norm, mla_kv_norm, gmlp_v_norm)
    sh1, sc1, g1, sh2, sc2, g2 = [mod[:, j] for j in range(N_MOD)]
    sh1c, sc1c, g1c, sh2c, sc2c, g2c = [mod_c[j] for j in range(N_MOD)]

    tm_lat = _pick_tile(L, (256, 128))
    tm_ctx = _pick_tile(C, (256, 128))
    lat = _proj_call(x, _stack_rows([norms[0], 1.0 + sc1, sh1], B, D), rope_lat, small, weights, tm_lat)
    cx = _proj_call(xc, _stack_rows([norms[0], 1.0 + sc1c, sh1c], B, D), rope_ctx, small, weights, tm_ctx)
    q, k, v, qn, qr, kn, vm, kr, u, vg, gates = lat
    qc, kc, vc, qnc, qrc, knc, vmc, krc, uc, vgc, gatesc = cx

    lat_keys = _mixer_keys((k, v, kn, kr, vm), B)
    ctx_keys = _mixer_keys((kc, vc, knc, krc, vmc), B)
    tq = _pick_tile(L, (512, 256, 128))
    gqa_o, mla_o = _attention_pair((q, qn, qr), lat_keys, ctx_keys, B, L, tq)

    ws = gmlp_w_s.astype(BF16)
    bs = jnp.repeat(gmlp_b_s.T, GMLP_GROUP_DIM, axis=1).astype(F32)
    wb = w_branch.astype(BF16)
    wo = w_out.astype(BF16)
    rw = jnp.pad(router_w, ((0, 0), (0, LANES - n_experts))).astype(F32)
    rb = jnp.pad(router_b, (0, LANES - n_experts)).reshape(1, LANES).astype(F32)

    tm_merge = _pick_tile(L, (256, 128))
    x1, tok, logits = _merge_call(x, _stack_rows([norms[1], g1, norms[2], 1.0 + sc2, sh2], B, D),
                                  gqa_o, mla_o, u, vg, gates, ws, bs, wb, wo, rw, rb, tm_merge)
    tok_all = tok.reshape(B * L, D)
    logits_all = logits.reshape(B * L, LANES)[:, :n_experts]
    if update_ctx:
        tc = _pick_tile(C, (256, 128))
        gqa_c, mla_c = _attention_pair((qc, qnc, qrc), ctx_keys, None, B, C, tc)
        xc1, tokc, logitsc = _merge_call(
            xc, _stack_rows([norms[1], g1c, norms[2], 1.0 + sc2c, sh2c], B, D),
            gqa_c, mla_c, uc, vgc, gatesc, ws, bs, wb, wo, rw, rb, tm_ctx)
        tok_all = jnp.concatenate([tok_all, tokc.reshape(B * C, D)], axis=0)
        logits_all = jnp.concatenate([logits_all, logitsc.reshape(B * C, LANES)[:, :n_experts]], axis=0)

    rows_per_block = 256
    top_w, dest, row_token, blk_expert, n_used = _route(logits_all, n_experts, rows_per_block)
    xb = jnp.take(tok_all, row_token, axis=0)
    y = _expert_call(blk_expert, n_used, xb, ew1, eb1, ew2, eb2, rows_per_block)

    x2 = _combine_call(x1, y, dest[:B * L], top_w[:B * L], _stack_rows([norms[3], g2], B, D), tm_merge)
    if update_ctx:
        xc = _combine_call(xc1, y, dest[B * L:], top_w[B * L:], _stack_rows([norms[3], g2c], B, D), tm_ctx)
    return x2, xc


def kernel(x, c, ctx, c_ctx, w_mod, b_mod, norm_gains, w_in, qk_norm, mla_q_norm, mla_kv_norm, mla_w_uq,
           mla_w_ukv, gmlp_v_norm, gmlp_w_s, gmlp_b_s, w_branch, w_out, router_w, router_b, expert_w1,
           expert_b1, expert_w2, expert_b2):
    B, L, D = x.shape
    C = ctx.shape[1]
    depth = w_mod.shape[0]
    rope_lat = _rope_tables(L)
    rope_ctx = _identity_rope(C)
    s_c = jax.nn.silu(c)
    s_cc = jax.nn.silu(c_ctx)
    xc = ctx
    for i in range(depth):
        mod = (jnp.dot(s_c, w_mod[i], precision=lax.Precision.HIGHEST) + b_mod[i]).reshape(B, N_MOD, D)
        mod_c = (jnp.dot(s_cc, w_mod[i], precision=lax.Precision.HIGHEST) + b_mod[i]).reshape(N_MOD, D)
        params = (norm_gains[i], w_in[i], qk_norm[i], mla_q_norm[i], mla_kv_norm[i], mla_w_uq[i],
                  mla_w_ukv[i], gmlp_v_norm[i], gmlp_w_s[i], gmlp_b_s[i], w_branch[i], w_out[i],
                  router_w[i], router_b[i], expert_w1[i], expert_b1[i], expert_w2[i], expert_b2[i])
        x, xc = _layer(x, xc, mod, mod_c, rope_lat, rope_ctx, params, update_ctx=(i < depth - 1))
    return x
```

```python
import functools
import math

import jax
import jax.numpy as jnp
import numpy as np
from jax import lax
from jax.experimental import pallas as pl
from jax.experimental.pallas import tpu as pltpu

GRID_W = 64
ROPE_THETA = 10000.0
RMS_EPS = 1e-6
N_MOD = 6
GQA_HEADS = 8
GQA_KV_HEADS = 2
GQA_HEAD_DIM = 64
MLA_HEADS = 8
MLA_NOPE_DIM = 64
MLA_ROPE_DIM = 32
MLA_V_DIM = 64
MLA_Q_RANK = 256
MLA_KV_RANK = 128
GMLP_GROUPS = 8
GMLP_GROUP_DIM = 64
GMLP_CHUNK = 128
N_BRANCHES = 3
TOP_K = 4
SWIGLU_ALPHA = 1.702
SWIGLU_LIMIT = 7.0

LANES = 128
VMEM_LIMIT_BYTES = 56 * 2**20
NEG_BIG = -1e30
ATTN_CHUNK = 1024
ATTN_LOOP_CHUNKS = 4
VT_EXTRA_ROWS = 16
LOG2E = math.log2(math.e)

BF16 = jnp.bfloat16
F32 = jnp.float32

GQA_Q_WIDTH = GQA_HEADS * GQA_HEAD_DIM
GQA_KV_WIDTH = GQA_KV_HEADS * GQA_HEAD_DIM
GMLP_WIDTH = GMLP_GROUPS * GMLP_GROUP_DIM
MLA_QN_WIDTH = MLA_HEADS * MLA_NOPE_DIM
MLA_QR_WIDTH = MLA_HEADS * MLA_ROPE_DIM
MLA_V_WIDTH = MLA_HEADS * MLA_V_DIM


def _pick_tile(n, candidates):
    for t in candidates:
        if n % t == 0:
            return t
    raise ValueError(f"no tile in {candidates} divides {n}")


def _full_spec(a):
    nd = a.ndim
    return pl.BlockSpec(a.shape, lambda *_: (0,) * nd)


def _row_rms(x, gain_row):
    ms = jnp.mean(x * x, axis=-1, keepdims=True)
    return x * lax.rsqrt(ms + RMS_EPS) * gain_row


def _group_rms(x, group_mean_mat, gain_row):
    ms = jnp.dot((x * x).astype(BF16), group_mean_mat, preferred_element_type=F32)
    return x * lax.rsqrt(ms + RMS_EPS) * gain_row


def _rope_lanes(x, cos_t, sin_t, half):
    rows, n = x.shape
    lane = lax.broadcasted_iota(jnp.int32, (rows, LANES), 1)
    first_half = (lane % (2 * half)) < half
    out = []
    for j in range(n // LANES):
        xb = x[:, j * LANES:(j + 1) * LANES]
        partner_up = pltpu.roll(xb, LANES - half, 1)
        partner_dn = pltpu.roll(xb, half, 1)
        partner = jnp.where(first_half, partner_up, partner_dn)
        out.append(xb * cos_t + partner * sin_t)
    return out[0] if len(out) == 1 else jnp.concatenate(out, axis=1)


def _proj_kernel(x_ref, mod_ref, rope_ref, small_ref,
                 wq_ref, wk_ref, wv_ref, wcq_ref, wckv_ref, wkr_ref, wuv_ref, wg_ref,
                 wuq_ref, wukv_ref, gq_ref, gk_ref,
                 q_ref, k_ref, v_ref, qn_ref, qr_ref, kn_ref, vm_ref, kr_ref, u_ref, vg_ref, g_ref):
    x = x_ref[0]
    mod = mod_ref[0]
    h = _row_rms(x, mod[0:1]) * mod[1:2] + mod[2:3]
    hb = h.astype(BF16)

    cos64, sin64 = rope_ref[0], rope_ref[1]
    cos32, sin32 = rope_ref[2], rope_ref[3]
    cos_kr, sin_kr = rope_ref[4], rope_ref[5]

    small = small_ref[...]
    qn_gain = small[0:1, :GQA_Q_WIDTH]
    kn_gain = small[1:2, :GQA_KV_WIDTH]
    cq_gain = small[2:3, :MLA_Q_RANK]
    ckv_gain = small[3:4, :MLA_KV_RANK]
    vg_gain = small[4:5, :GMLP_WIDTH]

    def proj(w_ref):
        return jnp.dot(hb, w_ref[...], preferred_element_type=F32)

    q = _group_rms(proj(wq_ref), gq_ref[...], qn_gain)
    q = _rope_lanes(q, cos64, sin64, GQA_HEAD_DIM // 2) * (GQA_HEAD_DIM ** -0.5 * LOG2E)
    q_ref[0] = q.astype(BF16)
    k = _group_rms(proj(wk_ref), gk_ref[...], kn_gain)
    k_ref[0] = _rope_lanes(k, cos64, sin64, GQA_HEAD_DIM // 2).astype(BF16)
    v_ref[0] = proj(wv_ref).astype(BF16)

    cq = _row_rms(proj(wcq_ref), cq_gain).astype(BF16)
    qm = jnp.dot(cq, wuq_ref[...], preferred_element_type=F32)
    mla_scale = (MLA_NOPE_DIM + MLA_ROPE_DIM) ** -0.5 * LOG2E
    qn_ref[0] = (qm[:, :MLA_QN_WIDTH] * mla_scale).astype(BF16)
    qr = _rope_lanes(qm[:, MLA_QN_WIDTH:], cos32, sin32, MLA_ROPE_DIM // 2) * mla_scale
    qr_ref[0] = qr.astype(BF16)
    ckv = _row_rms(proj(wckv_ref), ckv_gain).astype(BF16)
    kv = jnp.dot(ckv, wukv_ref[...], preferred_element_type=F32)
    kn_ref[0] = kv[:, :MLA_QN_WIDTH].astype(BF16)
    vm_ref[0] = kv[:, MLA_QN_WIDTH:].astype(BF16)
    kr_ref[0] = _rope_lanes(proj(wkr_ref), cos_kr, sin_kr, MLA_ROPE_DIM // 2).astype(BF16)

    act = jax.nn.gelu(proj(wuv_ref), approximate=True)
    u_ref[0] = act[:, :GMLP_WIDTH].astype(BF16)
    vg_ref[0] = _row_rms(act[:, GMLP_WIDTH:], vg_gain).astype(BF16)

    g_ref[0] = proj(wg_ref).astype(BF16)


def _proj_call(x, mod3, rope, small, weights, tm):
    B, L, D = x.shape
    widths = (GQA_Q_WIDTH, GQA_KV_WIDTH, GQA_KV_WIDTH, MLA_QN_WIDTH, MLA_QR_WIDTH, MLA_QN_WIDTH,
              MLA_V_WIDTH, LANES, GMLP_WIDTH, GMLP_WIDTH, N_BRANCHES * D)
    tok_spec = lambda w: pl.BlockSpec((1, tm, w), lambda b, i: (b, i, 0))
    in_specs = [tok_spec(D),
                pl.BlockSpec((1,) + mod3.shape[1:], lambda b, i: (b, 0, 0)),
                pl.BlockSpec((rope.shape[0], tm, LANES), lambda b, i: (0, i, 0)),
                _full_spec(small)] + [_full_spec(w) for w in weights]
    return pl.pallas_call(
        _proj_kernel,
        grid=(B, L // tm),
        in_specs=in_specs,
        out_specs=[tok_spec(w) for w in widths],
        out_shape=[jax.ShapeDtypeStruct((B, L, w), BF16) for w in widths],
        compiler_params=pltpu.CompilerParams(
            dimension_semantics=("parallel", "parallel"), vmem_limit_bytes=VMEM_LIMIT_BYTES),
        name="proj",
    )(x, mod3, rope, small, *weights)


def _attn_kernel(*refs, n_main, has_tail):
    if has_tail:
        (q_ref, k_ref, vt_ref, kt_ref, vtt_ref, o_ref,
         sa_ref, sb_ref, ca_ref, cb_ref, m_ref, acc_ref) = refs
    else:
        q_ref, k_ref, vt_ref, o_ref, sa_ref, sb_ref, ca_ref, cb_ref, m_ref, acc_ref = refs
        kt_ref = vtt_ref = None
    bufs = ((sa_ref, ca_ref), (sb_ref, cb_ref))
    dv = o_ref.shape[2]
    q = q_ref[0, 0]

    m_ref[...] = jnp.full(m_ref.shape, NEG_BIG, F32)
    acc_ref[...] = jnp.zeros(acc_ref.shape, F32)

    def scores(k_chunk, dst):
        st_ref, cmax_ref = dst
        st = lax.dot_general(k_chunk, q, (((1,), (1,)), ((), ())),
                             preferred_element_type=F32)
        st_ref[0:k_chunk.shape[0], :] = st
        cmax_ref[...] = jnp.max(st, axis=0, keepdims=True)

    def absorb(src, vt_chunk):
        st_ref, cmax_ref = src
        m = m_ref[...]
        m_new = jnp.maximum(m, cmax_ref[...])
        alpha = jnp.exp2(m - m_new)
        pt = jnp.exp2(st_ref[0:vt_chunk.shape[1], :] - m_new).astype(BF16)
        acc_ref[...] = alpha * acc_ref[...] + jnp.dot(vt_chunk, pt, preferred_element_type=F32)
        m_ref[...] = m_new

    main_k = lambda c: k_ref[0, 0, c]
    main_vt = lambda c: vt_ref[0, 0, c]

    scores(main_k(0), bufs[0])
    group = ATTN_LOOP_CHUNKS
    n_groups = (n_main - 1) // group
    if n_groups:
        def run_group(g, carry):
            for j in range(group):
                c = group * g + j
                scores(main_k(c + 1), bufs[(j + 1) % 2])
                absorb(bufs[j % 2], main_vt(c))
            return carry
        lax.fori_loop(0, n_groups, run_group, 0)
    rest = [(main_k, main_vt, c) for c in range(group * n_groups, n_main)]
    if has_tail:
        rest.append((lambda _: kt_ref[0, 0], lambda _: vtt_ref[0, 0], 0))
    for j, (k_of, vt_of, c) in enumerate(rest):
        if j + 1 < len(rest):
            nk_of, _, nc = rest[j + 1]
            scores(nk_of(nc), bufs[(j + 1) % 2])
        absorb(bufs[j % 2], vt_of(c))

    acc = acc_ref[...]
    o_ref[0, 0] = (acc[0:dv] / acc[dv:dv + 1]).astype(o_ref.dtype)


def _attn_call(q, k, vt, k_tail, vt_tail, k_head_of, v_head_of, tq):
    B, H, Lq, _ = q.shape
    _, _, n_main, kc, _ = k.shape
    dvx = vt.shape[3]
    dv = dvx - VT_EXTRA_ROWS
    has_tail = k_tail is not None
    in_specs = [pl.BlockSpec((1, 1, tq, LANES), lambda b, h, i: (b, h, i, 0)),
                pl.BlockSpec((1, 1, n_main, kc, LANES), lambda b, h, i: (b, k_head_of(h), 0, 0, 0)),
                pl.BlockSpec((1, 1, n_main, dvx, kc), lambda b, h, i: (b, v_head_of(h), 0, 0, 0))]
    args = [q, k, vt]
    if has_tail:
        kt = k_tail.shape[2]
        assert kt <= kc
        in_specs += [pl.BlockSpec((1, 1, kt, LANES), lambda b, h, i: (b, k_head_of(h), 0, 0)),
                     pl.BlockSpec((1, 1, dvx, kt), lambda b, h, i: (b, v_head_of(h), 0, 0))]
        args += [k_tail, vt_tail]
    return pl.pallas_call(
        functools.partial(_attn_kernel, n_main=n_main, has_tail=has_tail),
        grid=(B, H, Lq // tq),
        in_specs=in_specs,
        out_specs=pl.BlockSpec((1, 1, dv, tq), lambda b, h, i: (b, h, 0, i)),
        out_shape=jax.ShapeDtypeStruct((B, H, dv, Lq), BF16),
        scratch_shapes=[pltpu.VMEM((kc, tq), F32), pltpu.VMEM((kc, tq), F32),
                        pltpu.VMEM((1, tq), F32), pltpu.VMEM((1, tq), F32),
                        pltpu.VMEM((1, tq), F32), pltpu.VMEM((dvx, tq), F32)],
        compiler_params=pltpu.CompilerParams(
            dimension_semantics=("parallel", "parallel", "parallel"),
            vmem_limit_bytes=VMEM_LIMIT_BYTES),
        name="attn",
    )(*args)


def _merge_kernel(x_ref, mod_ref, ga_ref, ma_ref, u_ref, vg_ref, g_ref,
                  ws_ref, bs_ref, wb_ref, wo_ref, rw_ref, rb_ref,
                  xo_ref, tok_ref, lg_ref):
    tm = x_ref.shape[1]
    D = x_ref.shape[2]
    mod = mod_ref[0]

    lane = lax.broadcasted_iota(jnp.int32, (GMLP_CHUNK, LANES), 1)
    low_group = lane < GMLP_GROUP_DIM
    chunks = []
    for c in range(tm // GMLP_CHUNK):
        rows = slice(c * GMLP_CHUNK, (c + 1) * GMLP_CHUNK)
        cols = []
        for j in range(GMLP_WIDTH // LANES):
            vc = vg_ref[0, rows, j * LANES:(j + 1) * LANES]
            s_lo = jnp.dot(ws_ref[2 * j], vc, preferred_element_type=F32)
            s_hi = jnp.dot(ws_ref[2 * j + 1], vc, preferred_element_type=F32)
            cols.append(jnp.where(low_group, s_lo, s_hi))
        s = jnp.concatenate(cols, axis=1) + bs_ref[...]
        chunks.append((u_ref[0, rows, :].astype(F32) * s).astype(BF16))
    gm = chunks[0] if len(chunks) == 1 else jnp.concatenate(chunks, axis=0)

    branches = (ga_ref[0], ma_ref[0], gm)
    merged = None
    for i in range(N_BRANCHES):
        gate = jax.nn.sigmoid(g_ref[0, :, i * D:(i + 1) * D].astype(F32))
        term = gate * jnp.dot(branches[i], wb_ref[i], preferred_element_type=F32)
        merged = term if merged is None else merged + term
    y = jnp.dot(merged.astype(BF16), wo_ref[...], preferred_element_type=F32)
    x_new = x_ref[0] + mod[1:2] * _row_rms(y, mod[0:1])
    xo_ref[0] = x_new

    tok = (_row_rms(x_new, mod[2:3]) * mod[3:4] + mod[4:5]).astype(BF16)
    tok_ref[0] = tok
    lg_ref[0] = jnp.dot(tok, rw_ref[...], preferred_element_type=F32) + rb_ref[...]


def _merge_call(x, mod5, ga, ma, u, vg, g, ws, bs, wb, wo, rw, rb, tm):
    B, L, D = x.shape
    tok_spec = lambda w: pl.BlockSpec((1, tm, w), lambda b, i: (b, i, 0))
    consts = (ws, bs, wb, wo, rw, rb)
    return pl.pallas_call(
        _merge_kernel,
        grid=(B, L // tm),
        in_specs=[tok_spec(D), pl.BlockSpec((1,) + mod5.shape[1:], lambda b, i: (b, 0, 0)),
                  tok_spec(GQA_Q_WIDTH), tok_spec(MLA_V_WIDTH), tok_spec(GMLP_WIDTH),
                  tok_spec(GMLP_WIDTH), tok_spec(N_BRANCHES * D)] + [_full_spec(a) for a in consts],
        out_specs=[tok_spec(D), tok_spec(D), tok_spec(LANES)],
        out_shape=[jax.ShapeDtypeStruct((B, L, D), F32), jax.ShapeDtypeStruct((B, L, D), BF16),
                   jax.ShapeDtypeStruct((B, L, LANES), F32)],
        compiler_params=pltpu.CompilerParams(
            dimension_semantics=("parallel", "parallel"), vmem_limit_bytes=VMEM_LIMIT_BYTES),
        name="merge",
    )(x, mod5, ga, ma, u, vg, g, *consts)


def _expert_kernel(blk_expert_ref, n_used_ref, xb_ref, w1_ref, b1_ref, w2_ref, b2_ref, y_ref,
                   w1b_ref, w2b_ref):
    i = pl.program_id(0)
    d_expert = w2_ref.shape[1]
    new_expert = jnp.logical_or(i == 0, blk_expert_ref[i] != blk_expert_ref[jnp.maximum(i - 1, 0)])

    @pl.when(new_expert)
    def _():
        w1b_ref[...] = w1_ref[0].astype(BF16)
        w2b_ref[...] = w2_ref[0].astype(BF16)

    @pl.when(i < n_used_ref[0])
    def _():
        a = jnp.dot(xb_ref[...], w1b_ref[...], preferred_element_type=F32) + b1_ref[0]
        glu = jnp.minimum(a[:, :d_expert], SWIGLU_LIMIT)
        lin = jnp.clip(a[:, d_expert:], -SWIGLU_LIMIT, SWIGLU_LIMIT)
        act = glu * jax.nn.sigmoid(SWIGLU_ALPHA * glu) * (lin + 1.0)
        y_ref[...] = jnp.dot(act.astype(BF16), w2b_ref[...], preferred_element_type=F32) + b2_ref[0]

    @pl.when(i >= n_used_ref[0])
    def _():
        y_ref[...] = jnp.zeros(y_ref.shape, y_ref.dtype)


def _expert_call(blk_expert, n_used, xb, w1, b1, w2, b2, layer, rows_per_block):
    n_rows, D = xb.shape
    depth, E, _, two_de = w1.shape
    d_expert = w2.shape[2]
    grid_spec = pltpu.PrefetchScalarGridSpec(
        num_scalar_prefetch=2,
        grid=(n_rows // rows_per_block,),
        in_specs=[pl.BlockSpec((rows_per_block, D), lambda i, be, nu: (i, 0)),
                  pl.BlockSpec((None, 1, D, two_de), lambda i, be, nu: (layer, be[i], 0, 0)),
                  pl.BlockSpec((None, 1, 1, two_de), lambda i, be, nu: (layer, be[i], 0, 0)),
                  pl.BlockSpec((None, 1, d_expert, D), lambda i, be, nu: (layer, be[i], 0, 0)),
                  pl.BlockSpec((None, 1, 1, D), lambda i, be, nu: (layer, be[i], 0, 0))],
        out_specs=pl.BlockSpec((rows_per_block, D), lambda i, be, nu: (i, 0)),
        scratch_shapes=[pltpu.VMEM((D, two_de), BF16), pltpu.VMEM((d_expert, D), BF16)],
    )
    return pl.pallas_call(
        _expert_kernel,
        grid_spec=grid_spec,
        out_shape=jax.ShapeDtypeStruct((n_rows, D), F32),
        compiler_params=pltpu.CompilerParams(
            dimension_semantics=("arbitrary",), vmem_limit_bytes=VMEM_LIMIT_BYTES),
        name="experts",
    )(blk_expert, n_used, xb, w1, b1.reshape(depth, E, 1, two_de), w2, b2.reshape(depth, E, 1, D))


def _combine_kernel(dcur_ref, dnxt_ref, y_hbm, w_ref, x_ref, mod_ref, o_ref, gbuf, sem, *, n_tiles):
    s = pl.program_id(0)
    last = n_tiles - 1
    tm = x_ref.shape[0]
    slot = s % 2

    def row_copy(row, k, r, slot_):
        return pltpu.make_async_copy(y_hbm.at[pl.ds(row, 1)], gbuf.at[slot_, k, pl.ds(r, 1)],
                                     sem.at[slot_])

    def start_gather(d_ref, slot_):
        def token(r, carry):
            for k in range(TOP_K):
                row_copy(d_ref[0, 0, r * TOP_K + k], k, r, slot_).start()
            return carry
        lax.fori_loop(0, tm, token, 0)

    @pl.when(s == 0)
    def _():
        start_gather(dcur_ref, 0)

    @pl.when(s < last)
    def _():
        start_gather(dnxt_ref, 1 - slot)

    for k in range(TOP_K):
        pltpu.make_async_copy(y_hbm.at[pl.ds(0, tm)], gbuf.at[slot, k], sem.at[slot]).wait()

    w = w_ref[...]
    f = w[:, 0:1] * gbuf[slot, 0]
    for k in range(1, TOP_K):
        f = f + w[:, k:k + 1] * gbuf[slot, k]
    mod = mod_ref[0]
    o_ref[...] = x_ref[...] + mod[1:2] * _row_rms(f, mod[0:1])


def _combine_call(x, y, dest, top_w, mod2, tm):
    B, L, D = x.shape
    n_tiles = B * L // tm
    tiles_per_mod = n_tiles // mod2.shape[0]
    dest3 = dest.reshape(n_tiles, 1, tm * TOP_K)
    tok_spec = pl.BlockSpec((tm, D), lambda s: (s, 0))
    smem_spec = lambda idx: pl.BlockSpec((1, 1, tm * TOP_K), idx, memory_space=pltpu.SMEM)
    out = pl.pallas_call(
        functools.partial(_combine_kernel, n_tiles=n_tiles),
        grid=(n_tiles,),
        in_specs=[smem_spec(lambda s: (s, 0, 0)),
                  smem_spec(lambda s: (jnp.minimum(s + 1, n_tiles - 1), 0, 0)),
                  pl.BlockSpec(memory_space=pl.ANY),
                  pl.BlockSpec((tm, TOP_K), lambda s: (s, 0)),
                  tok_spec,
                  pl.BlockSpec((1,) + mod2.shape[1:], lambda s: (s // tiles_per_mod, 0, 0))],
        out_specs=tok_spec,
        out_shape=jax.ShapeDtypeStruct((B * L, D), F32),
        scratch_shapes=[pltpu.VMEM((2, TOP_K, tm, D), y.dtype), pltpu.SemaphoreType.DMA((2,))],
        compiler_params=pltpu.CompilerParams(
            dimension_semantics=("arbitrary",), vmem_limit_bytes=VMEM_LIMIT_BYTES),
        name="combine",
    )(dest3, dest3, y, top_w, x.reshape(B * L, D), mod2)
    return out.reshape(B, L, D)


def _rope_tables(n_lat, dtype=F32):
    rows = n_lat // GRID_W
    row = jnp.repeat(jnp.arange(rows, dtype=F32), GRID_W)
    col = jnp.tile(jnp.arange(GRID_W, dtype=F32), rows)

    def tables(rot_dim):
        quarter = rot_dim // 4
        inv_freq = ROPE_THETA ** (-jnp.arange(quarter, dtype=F32) / quarter)
        ang = jnp.concatenate([row[:, None] * inv_freq, col[:, None] * inv_freq], axis=-1)
        cos = jnp.concatenate([jnp.cos(ang), jnp.cos(ang)], axis=-1)
        sin = jnp.concatenate([-jnp.sin(ang), jnp.sin(ang)], axis=-1)
        return cos, sin

    cos64, sin64 = tables(GQA_HEAD_DIM)
    cos32, sin32 = tables(MLA_ROPE_DIM)
    tile = lambda t: jnp.tile(t, (1, LANES // t.shape[1]))
    pad_one = lambda t: jnp.concatenate([t, jnp.ones((n_lat, LANES - t.shape[1]), F32)], axis=1)
    pad_zero = lambda t: jnp.concatenate([t, jnp.zeros((n_lat, LANES - t.shape[1]), F32)], axis=1)
    return jnp.stack([tile(cos64), tile(sin64), tile(cos32), tile(sin32),
                      pad_one(cos32), pad_zero(sin32)]).astype(dtype)


def _identity_rope(n):
    one, zero = jnp.ones((n, LANES), F32), jnp.zeros((n, LANES), F32)
    return jnp.stack([one, zero, one, zero, one, zero])


def _group_mean_matrix(width, group):
    idx = np.arange(width) // group
    return jnp.asarray((idx[:, None] == idx[None, :]).astype(np.float32) / group, dtype=BF16)


def _layer_weights(w_in, mla_w_uq, mla_w_ukv, D):
    splits = np.cumsum((GQA_Q_WIDTH, GQA_KV_WIDTH, GQA_KV_WIDTH, MLA_Q_RANK, MLA_KV_RANK, MLA_ROPE_DIM,
                        2 * GMLP_WIDTH, N_BRANCHES * D))[:-1].tolist()
    wq, wk, wv, wcq, wckv, wkr, wuv, wg = jnp.split(w_in.astype(BF16), splits, axis=1)
    wkr = jnp.pad(wkr, ((0, 0), (0, LANES - MLA_ROPE_DIM)))
    uq = mla_w_uq.astype(BF16).reshape(MLA_Q_RANK, MLA_HEADS, MLA_NOPE_DIM + MLA_ROPE_DIM)
    wuq = jnp.concatenate([uq[:, :, :MLA_NOPE_DIM].reshape(MLA_Q_RANK, -1),
                           uq[:, :, MLA_NOPE_DIM:].reshape(MLA_Q_RANK, -1)], axis=1)
    ukv = mla_w_ukv.astype(BF16).reshape(MLA_KV_RANK, MLA_HEADS, MLA_NOPE_DIM + MLA_V_DIM)
    wukv = jnp.concatenate([ukv[:, :, :MLA_NOPE_DIM].reshape(MLA_KV_RANK, -1),
                            ukv[:, :, MLA_NOPE_DIM:].reshape(MLA_KV_RANK, -1)], axis=1)
    return (wq, wk, wv, wcq, wckv, wkr, wuv, wg, wuq, wukv,
            _group_mean_matrix(GQA_Q_WIDTH, GQA_HEAD_DIM), _group_mean_matrix(GQA_KV_WIDTH, GQA_HEAD_DIM))


def _small_params(qk_norm, mla_q_norm, mla_kv_norm, gmlp_v_norm):
    width = max(GQA_Q_WIDTH, GMLP_WIDTH)
    row = lambda v: jnp.pad(v, (0, width - v.shape[0]))
    rows = [row(jnp.tile(qk_norm[0], GQA_HEADS)), row(jnp.tile(qk_norm[1], GQA_KV_HEADS)),
            row(mla_q_norm), row(mla_kv_norm), row(gmlp_v_norm)]
    rows += [jnp.zeros((width,), F32)] * (8 - len(rows))
    return jnp.stack(rows).astype(F32)


def _stack_rows(rows, B, D):
    full = [jnp.broadcast_to(r, (B, D)) for r in rows]
    full += [jnp.zeros((B, D), F32)] * (8 - len(full))
    return jnp.stack(full, axis=1).astype(F32)


def _heads_to_slabs(parts, B, Lx, H):
    pieces = [p.reshape(B, Lx, H, p.shape[2] // H) for p in parts]
    used = sum(p.shape[3] for p in pieces)
    if used < LANES:
        pieces.append(jnp.zeros((B, Lx, H, LANES - used), pieces[0].dtype))
    return jnp.transpose(jnp.concatenate(pieces, axis=3), (0, 2, 1, 3))


def _gqa_q_slabs(q, B, Lx):
    qh = jnp.transpose(q.reshape(B, Lx, GQA_HEADS, GQA_HEAD_DIM), (0, 2, 1, 3))
    group = GQA_HEADS // GQA_KV_HEADS
    zeros = jnp.zeros_like(qh)
    in_low = (jnp.arange(GQA_HEADS) // group == 0)[None, :, None, None]
    return jnp.concatenate([jnp.where(in_low, qh, zeros), jnp.where(in_low, zeros, qh)], axis=3)


def _heads_last(ot, B, Lx):
    return jnp.transpose(ot, (0, 3, 1, 2)).reshape(B, Lx, ot.shape[1] * ot.shape[2])


def _values_transposed(v, B, Lk, H):
    vt = jnp.transpose(v, (0, 2, 1)).reshape(B, H, v.shape[2] // H, Lk)
    extra = jnp.zeros((B, H, VT_EXTRA_ROWS, Lk), vt.dtype).at[:, :, 0].set(1)
    return jnp.concatenate([vt, extra], axis=2)


def _chunk_keys(k, vt, kc):
    B, H, Lk, _ = k.shape
    dv = vt.shape[2]
    n = Lk // kc
    return (k.reshape(B, H, n, kc, LANES),
            jnp.transpose(vt.reshape(B, vt.shape[1], dv, n, kc), (0, 1, 3, 2, 4)))


def _mixer_keys(keys, B):
    k, v, kn, kr, vm = keys
    Lk = k.shape[1]
    kr_rep = jnp.broadcast_to(kr[:, :, None, :MLA_ROPE_DIM], (B, Lk, MLA_HEADS, MLA_ROPE_DIM))
    return dict(
        gqa_k=k[:, None], gqa_vt=_values_transposed(v, B, Lk, GQA_KV_HEADS),
        mla_k=_heads_to_slabs([kn, kr_rep.reshape(B, Lk, MLA_QR_WIDTH)], B, Lk, MLA_HEADS),
        mla_vt=_values_transposed(vm, B, Lk, MLA_HEADS))


def _attention_pair(pq, main, tail, B, Lq, tq):
    q, qn, qr = pq
    kc = _pick_tile(main["gqa_k"].shape[2], (ATTN_CHUNK, 512, 256, 128))
    group = GQA_HEADS // GQA_KV_HEADS
    tail_of = lambda name: None if tail is None else tail[name]
    gk, gvt = _chunk_keys(main["gqa_k"], main["gqa_vt"], kc)
    gqa = _attn_call(_gqa_q_slabs(q, B, Lq), gk, gvt, tail_of("gqa_k"), tail_of("gqa_vt"),
                     lambda h: 0, lambda h: h // group, tq)
    mk, mvt = _chunk_keys(main["mla_k"], main["mla_vt"], kc)
    mla = _attn_call(_heads_to_slabs([qn, qr], B, Lq, MLA_HEADS), mk, mvt,
                     tail_of("mla_k"), tail_of("mla_vt"), lambda h: h, lambda h: h, tq)
    return _heads_last(gqa, B, Lq), _heads_last(mla, B, Lq)


def _route(logits, n_experts, rows_per_block):
    T = logits.shape[0]
    P = T * TOP_K
    top_logit, top_idx = lax.top_k(logits, TOP_K)
    top_w = jax.nn.softmax(top_logit, axis=-1)
    e_flat = top_idx.reshape(P)
    onehot = (e_flat[:, None] == jnp.arange(n_experts, dtype=e_flat.dtype)[None, :]).astype(jnp.int32)
    csum = jnp.cumsum(onehot, axis=0)
    rank = jnp.sum(onehot * (csum - 1), axis=1)
    counts = csum[-1]
    padded = (counts + rows_per_block - 1) // rows_per_block * rows_per_block
    pad_end = jnp.cumsum(padded)
    pad_start = pad_end - padded
    dest = pad_start[e_flat] + rank
    n_blocks = -(-(P + n_experts * (rows_per_block - 1)) // rows_per_block)
    blk_start = jnp.arange(n_blocks, dtype=jnp.int32) * rows_per_block
    blk_expert = jnp.minimum(jnp.sum((pad_end[None, :] <= blk_start[:, None]).astype(jnp.int32), axis=1),
                             n_experts - 1)
    n_used = (pad_end[-1] // rows_per_block).astype(jnp.int32).reshape(1)
    bits = max(1, (P - 1).bit_length())
    pair_sorted = jnp.sort((e_flat.astype(jnp.int32) << bits) | jnp.arange(P, dtype=jnp.int32)) & ((1 << bits) - 1)
    e_row = jnp.repeat(blk_expert, rows_per_block)
    within = jnp.arange(n_blocks * rows_per_block, dtype=jnp.int32) - pad_start[e_row]
    live = within < counts[e_row]
    grp_start = jnp.cumsum(counts) - counts
    row_token = jnp.where(live, pair_sorted[jnp.where(live, grp_start[e_row] + within, 0)] // TOP_K, 0)
    return top_w, dest.reshape(T, TOP_K), row_token, blk_expert, n_used


def _layer(x, xc, mod, mod_c, rope_lat, rope_ctx, p, layer, update_ctx):
    B, L, D = x.shape
    C = xc.shape[1]
    (norms, w_in, qk_norm, mla_q_norm, mla_kv_norm, mla_w_uq, mla_w_ukv, gmlp_v_norm, gmlp_w_s,
     gmlp_b_s, w_branch, w_out, router_w, router_b, ew1, eb1, ew2, eb2) = p
    n_experts = router_w.shape[1]

    weights = _layer_weights(w_in, mla_w_uq, mla_w_ukv, D)
    small = _small_params(qk_norm, mla_q_norm, mla_kv_norm, gmlp_v_norm)
    sh1, sc1, g1, sh2, sc2, g2 = [mod[:, j] for j in range(N_MOD)]
    sh1c, sc1c, g1c, sh2c, sc2c, g2c = [mod_c[j] for j in range(N_MOD)]

    tm_lat = _pick_tile(L, (256, 128))
    tm_ctx = _pick_tile(C, (256, 128))
    lat = _proj_call(x, _stack_rows([norms[0], 1.0 + sc1, sh1], B, D), rope_lat, small, weights, tm_lat)
    cx = _proj_call(xc, _stack_rows([norms[0], 1.0 + sc1c, sh1c], B, D), rope_ctx, small, weights, tm_ctx)
    q, k, v, qn, qr, kn, vm, kr, u, vg, gates = lat
    qc, kc, vc, qnc, qrc, knc, vmc, krc, uc, vgc, gatesc = cx

    lat_keys = _mixer_keys((k, v, kn, kr, vm), B)
    ctx_keys = _mixer_keys((kc, vc, knc, krc, vmc), B)
    tq = _pick_tile(L, (512, 256, 128))
    gqa_o, mla_o = _attention_pair((q, qn, qr), lat_keys, ctx_keys, B, L, tq)

    ws = gmlp_w_s.astype(BF16)
    bs = jnp.repeat(gmlp_b_s.T, GMLP_GROUP_DIM, axis=1).astype(F32)
    wb = w_branch.astype(BF16)
    wo = w_out.astype(BF16)
    rw = jnp.pad(router_w, ((0, 0), (0, LANES - n_experts))).astype(BF16)
    rb = jnp.pad(router_b, (0, LANES - n_experts)).reshape(1, LANES).astype(F32)

    tm_merge = _pick_tile(L, (512, 256, 128))
    x1, tok, logits = _merge_call(x, _stack_rows([norms[1], g1, norms[2], 1.0 + sc2, sh2], B, D),
                                  gqa_o, mla_o, u, vg, gates, ws, bs, wb, wo, rw, rb, tm_merge)
    tok_all = tok.reshape(B * L, D)
    logits_all = logits.reshape(B * L, LANES)[:, :n_experts]
    if update_ctx:
        tc = _pick_tile(C, (256, 128))
        gqa_c, mla_c = _attention_pair((qc, qnc, qrc), ctx_keys, None, B, C, tc)
        xc1, tokc, logitsc = _merge_call(
            xc, _stack_rows([norms[1], g1c, norms[2], 1.0 + sc2c, sh2c], B, D),
            gqa_c, mla_c, uc, vgc, gatesc, ws, bs, wb, wo, rw, rb, tm_ctx)
        tok_all = jnp.concatenate([tok_all, tokc.reshape(B * C, D)], axis=0)
        logits_all = jnp.concatenate([logits_all, logitsc.reshape(B * C, LANES)[:, :n_experts]], axis=0)

    rows_per_block = 256
    top_w, dest, row_token, blk_expert, n_used = _route(logits_all, n_experts, rows_per_block)
    xb = jnp.take(tok_all, row_token, axis=0)
    y = _expert_call(blk_expert, n_used, xb, ew1, eb1, ew2, eb2, layer, rows_per_block)

    x2 = _combine_call(x1, y, dest[:B * L], top_w[:B * L], _stack_rows([norms[3], g2], B, D), tm_merge)
    if update_ctx:
        xc = _combine_call(xc1, y, dest[B * L:], top_w[B * L:], _stack_rows([norms[3], g2c], B, D), tm_ctx)
    return x2, xc


def kernel(x, c, ctx, c_ctx, w_mod, b_mod, norm_gains, w_in, qk_norm, mla_q_norm, mla_kv_norm, mla_w_uq,
           mla_w_ukv, gmlp_v_norm, gmlp_w_s, gmlp_b_s, w_branch, w_out, router_w, router_b, expert_w1,
           expert_b1, expert_w2, expert_b2):
    B, L, D = x.shape
    C = ctx.shape[1]
    depth = w_mod.shape[0]
    rope_lat = _rope_tables(L)
    rope_ctx = _identity_rope(C)
    s_c = jax.nn.silu(c)
    s_cc = jax.nn.silu(c_ctx)
    xc = ctx
    for i in range(depth):
        mod = (jnp.dot(s_c, w_mod[i], precision=lax.Precision.HIGHEST) + b_mod[i]).reshape(B, N_MOD, D)
        mod_c = (jnp.dot(s_cc, w_mod[i], precision=lax.Precision.HIGHEST) + b_mod[i]).reshape(N_MOD, D)
        params = (norm_gains[i], w_in[i], qk_norm[i], mla_q_norm[i], mla_kv_norm[i], mla_w_uq[i],
                  mla_w_ukv[i], gmlp_v_norm[i], gmlp_w_s[i], gmlp_b_s[i], w_branch[i], w_out[i],
                  router_w[i], router_b[i], expert_w1, expert_b1, expert_w2, expert_b2)
        x, xc = _layer(x, xc, mod, mod_c, rope_lat, rope_ctx, params, i, update_ctx=(i < depth - 1))
    return x
```

```python
import functools
import math

import jax
import jax.numpy as jnp
import numpy as np
from jax import lax
from jax.experimental import pallas as pl
from jax.experimental.pallas import tpu as pltpu

GRID_W = 64
ROPE_THETA = 10000.0
RMS_EPS = 1e-6
N_MOD = 6
GQA_HEADS = 8
GQA_KV_HEADS = 2
GQA_HEAD_DIM = 64
MLA_HEADS = 8
MLA_NOPE_DIM = 64
MLA_ROPE_DIM = 32
MLA_V_DIM = 64
MLA_Q_RANK = 256
MLA_KV_RANK = 128
GMLP_GROUPS = 8
GMLP_GROUP_DIM = 64
GMLP_CHUNK = 128
N_BRANCHES = 3
TOP_K = 4
SWIGLU_ALPHA = 1.702
SWIGLU_LIMIT = 7.0

LANES = 128
VMEM_LIMIT_BYTES = 56 * 2**20
NEG_BIG = -1e30
ATTN_CHUNK = 1024
ATTN_LOOP_CHUNKS = 4
VT_EXTRA_ROWS = 16
LOG2E = math.log2(math.e)

BF16 = jnp.bfloat16
F32 = jnp.float32

GQA_Q_WIDTH = GQA_HEADS * GQA_HEAD_DIM
GQA_KV_WIDTH = GQA_KV_HEADS * GQA_HEAD_DIM
GMLP_WIDTH = GMLP_GROUPS * GMLP_GROUP_DIM
MLA_QN_WIDTH = MLA_HEADS * MLA_NOPE_DIM
MLA_QR_WIDTH = MLA_HEADS * MLA_ROPE_DIM
MLA_V_WIDTH = MLA_HEADS * MLA_V_DIM


def _pick_tile(n, candidates):
    for t in candidates:
        if n % t == 0:
            return t
    raise ValueError(f"no tile in {candidates} divides {n}")


def _full_spec(a):
    nd = a.ndim
    return pl.BlockSpec(a.shape, lambda *_: (0,) * nd)


def _row_rms(x, gain_row):
    ms = jnp.mean(x * x, axis=-1, keepdims=True)
    return x * lax.rsqrt(ms + RMS_EPS) * gain_row


def _group_rms(x, group_mean_mat, gain_row):
    ms = jnp.dot((x * x).astype(BF16), group_mean_mat, preferred_element_type=F32)
    return x * lax.rsqrt(ms + RMS_EPS) * gain_row


def _rope_lanes(x, cos_t, sin_t, half):
    rows, n = x.shape
    lane = lax.broadcasted_iota(jnp.int32, (rows, LANES), 1)
    first_half = (lane % (2 * half)) < half
    out = []
    for j in range(n // LANES):
        xb = x[:, j * LANES:(j + 1) * LANES]
        partner_up = pltpu.roll(xb, LANES - half, 1)
        partner_dn = pltpu.roll(xb, half, 1)
        partner = jnp.where(first_half, partner_up, partner_dn)
        out.append(xb * cos_t + partner * sin_t)
    return out[0] if len(out) == 1 else jnp.concatenate(out, axis=1)


def _proj_kernel(x_ref, mod_ref, rope_ref, small_ref,
                 wq_ref, wk_ref, wv_ref, wcq_ref, wckv_ref, wkr_ref, wuv_ref, wg_ref,
                 wuq_ref, wukv_ref, gq_ref, gk_ref,
                 qg_ref, k_ref, vtg_ref, qm_ref, km_ref, vtm_ref, u_ref, vg_ref, g_ref):
    tm = x_ref.shape[1]
    x = x_ref[0]
    mod = mod_ref[0]
    h = _row_rms(x, mod[0:1]) * mod[1:2] + mod[2:3]
    hb = h.astype(BF16)

    cos64, sin64 = rope_ref[0], rope_ref[1]
    cos32, sin32 = rope_ref[2], rope_ref[3]
    cos_kr, sin_kr = rope_ref[4], rope_ref[5]

    small = small_ref[...]
    qn_gain = small[0:1, :GQA_Q_WIDTH]
    kn_gain = small[1:2, :GQA_KV_WIDTH]
    cq_gain = small[2:3, :MLA_Q_RANK]
    ckv_gain = small[3:4, :MLA_KV_RANK]
    vg_gain = small[4:5, :GMLP_WIDTH]

    def proj(w_ref):
        return jnp.dot(hb, w_ref[...], preferred_element_type=F32)

    lane = lax.broadcasted_iota(jnp.int32, (tm, LANES), 1)
    half = LANES // 2
    block = lambda a, j: a[:, j * LANES:(j + 1) * LANES]
    to_lanes = lambda a, shift: a if shift % LANES == 0 else pltpu.roll(a, shift % LANES, 1)

    def store_values_t(v_block, dst_ref, first_head):
        vt = v_block.T
        extra_row = lax.broadcasted_iota(jnp.int32, (VT_EXTRA_ROWS, tm), 0)
        ones_then_zeros = jnp.where(extra_row == 0, 1.0, 0.0).astype(BF16)
        for i in range(2):
            dst_ref[0, first_head + i, 0, 0:half, :] = vt[i * half:(i + 1) * half].astype(BF16)
            dst_ref[0, first_head + i, 0, half:half + VT_EXTRA_ROWS, :] = ones_then_zeros

    q = _group_rms(proj(wq_ref), gq_ref[...], qn_gain)
    q = _rope_lanes(q, cos64, sin64, GQA_HEAD_DIM // 2) * (GQA_HEAD_DIM ** -0.5 * LOG2E)
    group = GQA_HEADS // GQA_KV_HEADS
    for hd in range(GQA_HEADS):
        src_half, dst_half = hd % 2, hd // group
        part = to_lanes(block(q, hd // 2), (dst_half - src_half) * half)
        keep = (lane >= half) if dst_half else (lane < half)
        qg_ref[0, hd] = jnp.where(keep, part, 0.0).astype(BF16)
    k = _group_rms(proj(wk_ref), gk_ref[...], kn_gain)
    k_ref[0] = _rope_lanes(k, cos64, sin64, GQA_HEAD_DIM // 2).astype(BF16)
    store_values_t(proj(wv_ref), vtg_ref, 0)

    cq = _row_rms(proj(wcq_ref), cq_gain).astype(BF16)
    qm = jnp.dot(cq, wuq_ref[...], preferred_element_type=F32)
    mla_scale = (MLA_NOPE_DIM + MLA_ROPE_DIM) ** -0.5 * LOG2E
    qn = qm[:, :MLA_QN_WIDTH] * mla_scale
    qr = _rope_lanes(qm[:, MLA_QN_WIDTH:], cos32, sin32, MLA_ROPE_DIM // 2) * mla_scale
    ckv = _row_rms(proj(wckv_ref), ckv_gain).astype(BF16)
    kv = jnp.dot(ckv, wukv_ref[...], preferred_element_type=F32)
    kn = kv[:, :MLA_QN_WIDTH]
    kr = _rope_lanes(proj(wkr_ref), cos_kr, sin_kr, MLA_ROPE_DIM // 2)
    kr_part = to_lanes(kr, MLA_NOPE_DIM)
    ropes_per_block = LANES // MLA_ROPE_DIM
    in_nope = lane < MLA_NOPE_DIM
    in_rope = lane < MLA_NOPE_DIM + MLA_ROPE_DIM
    for hd in range(MLA_HEADS):
        q_nope = to_lanes(block(qn, hd // 2), -(hd % 2) * half)
        q_rope = to_lanes(block(qr, hd // ropes_per_block),
                          MLA_NOPE_DIM - (hd % ropes_per_block) * MLA_ROPE_DIM)
        qm_ref[0, hd] = jnp.where(in_nope, q_nope, jnp.where(in_rope, q_rope, 0.0)).astype(BF16)
        k_nope = to_lanes(block(kn, hd // 2), -(hd % 2) * half)
        km_ref[0, hd] = jnp.where(in_nope, k_nope, jnp.where(in_rope, kr_part, 0.0)).astype(BF16)
    for j in range(MLA_V_WIDTH // LANES):
        store_values_t(block(kv, MLA_QN_WIDTH // LANES + j), vtm_ref, 2 * j)

    act = jax.nn.gelu(proj(wuv_ref), approximate=True)
    u_ref[0] = act[:, :GMLP_WIDTH].astype(BF16)
    vg_ref[0] = _row_rms(act[:, GMLP_WIDTH:], vg_gain).astype(BF16)

    g_ref[0] = proj(wg_ref).astype(BF16)


def _proj_call(x, mod3, rope, small, weights, tm, kc):
    B, L, D = x.shape
    per_chunk = kc // tm
    dvx = GQA_HEAD_DIM + VT_EXTRA_ROWS
    tok_spec = lambda w: pl.BlockSpec((1, tm, w), lambda b, i: (b, i, 0))
    slab_spec = lambda H: pl.BlockSpec((1, H, tm, LANES), lambda b, i: (b, 0, i, 0))
    vt_spec = lambda H: pl.BlockSpec((1, H, 1, dvx, tm),
                                     lambda b, i: (b, 0, i // per_chunk, 0, i % per_chunk))
    tok_shape = lambda w: jax.ShapeDtypeStruct((B, L, w), BF16)
    slab_shape = lambda H: jax.ShapeDtypeStruct((B, H, L, LANES), BF16)
    vt_shape = lambda H: jax.ShapeDtypeStruct((B, H, L // kc, dvx, kc), BF16)
    in_specs = [tok_spec(D),
                pl.BlockSpec((1,) + mod3.shape[1:], lambda b, i: (b, 0, 0)),
                pl.BlockSpec((rope.shape[0], tm, LANES), lambda b, i: (0, i, 0)),
                _full_spec(small)] + [_full_spec(w) for w in weights]
    names = ("qg", "k", "vtg", "qm", "km", "vtm", "u", "vg", "gates")
    out_specs = [slab_spec(GQA_HEADS), tok_spec(GQA_KV_WIDTH), vt_spec(GQA_KV_HEADS),
                 slab_spec(MLA_HEADS), slab_spec(MLA_HEADS), vt_spec(MLA_HEADS),
                 tok_spec(GMLP_WIDTH), tok_spec(GMLP_WIDTH), tok_spec(N_BRANCHES * D)]
    out_shape = [slab_shape(GQA_HEADS), tok_shape(GQA_KV_WIDTH), vt_shape(GQA_KV_HEADS),
                 slab_shape(MLA_HEADS), slab_shape(MLA_HEADS), vt_shape(MLA_HEADS),
                 tok_shape(GMLP_WIDTH), tok_shape(GMLP_WIDTH), tok_shape(N_BRANCHES * D)]
    outs = pl.pallas_call(
        _proj_kernel,
        grid=(B, L // tm),
        in_specs=in_specs,
        out_specs=out_specs,
        out_shape=out_shape,
        compiler_params=pltpu.CompilerParams(
            dimension_semantics=("parallel", "parallel"), vmem_limit_bytes=VMEM_LIMIT_BYTES),
        name="proj",
    )(x, mod3, rope, small, *weights)
    return dict(zip(names, outs))


def _attn_kernel(*refs, n_main, has_tail):
    if has_tail:
        (q_ref, k_ref, vt_ref, kt_ref, vtt_ref, o_ref,
         sa_ref, sb_ref, ca_ref, cb_ref, m_ref, acc_ref) = refs
    else:
        q_ref, k_ref, vt_ref, o_ref, sa_ref, sb_ref, ca_ref, cb_ref, m_ref, acc_ref = refs
        kt_ref = vtt_ref = None
    bufs = ((sa_ref, ca_ref), (sb_ref, cb_ref))
    dv = o_ref.shape[2]
    q = q_ref[0, 0]

    m_ref[...] = jnp.full(m_ref.shape, NEG_BIG, F32)
    acc_ref[...] = jnp.zeros(acc_ref.shape, F32)

    def scores(k_chunk, dst):
        st_ref, cmax_ref = dst
        st = lax.dot_general(k_chunk, q, (((1,), (1,)), ((), ())),
                             preferred_element_type=F32)
        st_ref[0:k_chunk.shape[0], :] = st
        cmax_ref[...] = jnp.max(st, axis=0, keepdims=True)

    def absorb(src, vt_chunk):
        st_ref, cmax_ref = src
        m = m_ref[...]
        m_new = jnp.maximum(m, cmax_ref[...])
        alpha = jnp.exp2(m - m_new)
        pt = jnp.exp2(st_ref[0:vt_chunk.shape[1], :] - m_new).astype(BF16)
        acc_ref[...] = alpha * acc_ref[...] + jnp.dot(vt_chunk, pt, preferred_element_type=F32)
        m_ref[...] = m_new

    main_k = lambda c: k_ref[0, 0, c]
    main_vt = lambda c: vt_ref[0, 0, c]

    scores(main_k(0), bufs[0])
    group = ATTN_LOOP_CHUNKS
    n_groups = (n_main - 1) // group
    if n_groups:
        def run_group(g, carry):
            for j in range(group):
                c = group * g + j
                scores(main_k(c + 1), bufs[(j + 1) % 2])
                absorb(bufs[j % 2], main_vt(c))
            return carry
        lax.fori_loop(0, n_groups, run_group, 0)
    rest = [(main_k, main_vt, c) for c in range(group * n_groups, n_main)]
    if has_tail:
        rest.append((lambda _: kt_ref[0, 0], lambda _: vtt_ref[0, 0], 0))
    for j, (k_of, vt_of, c) in enumerate(rest):
        if j + 1 < len(rest):
            nk_of, _, nc = rest[j + 1]
            scores(nk_of(nc), bufs[(j + 1) % 2])
        absorb(bufs[j % 2], vt_of(c))

    acc = acc_ref[...]
    o_ref[0, 0] = (acc[0:dv] / acc[dv:dv + 1]).astype(o_ref.dtype)


def _attn_call(q, k, vt, k_tail, vt_tail, k_head_of, v_head_of, tq):
    B, H, Lq, _ = q.shape
    _, _, n_main, kc, _ = k.shape
    dvx = vt.shape[3]
    dv = dvx - VT_EXTRA_ROWS
    has_tail = k_tail is not None
    in_specs = [pl.BlockSpec((1, 1, tq, LANES), lambda b, h, i: (b, h, i, 0)),
                pl.BlockSpec((1, 1, n_main, kc, LANES), lambda b, h, i: (b, k_head_of(h), 0, 0, 0)),
                pl.BlockSpec((1, 1, n_main, dvx, kc), lambda b, h, i: (b, v_head_of(h), 0, 0, 0))]
    args = [q, k, vt]
    if has_tail:
        kt = k_tail.shape[2]
        assert kt <= kc
        in_specs += [pl.BlockSpec((1, 1, kt, LANES), lambda b, h, i: (b, k_head_of(h), 0, 0)),
                     pl.BlockSpec((1, 1, dvx, kt), lambda b, h, i: (b, v_head_of(h), 0, 0))]
        args += [k_tail, vt_tail]
    return pl.pallas_call(
        functools.partial(_attn_kernel, n_main=n_main, has_tail=has_tail),
        grid=(B, H, Lq // tq),
        in_specs=in_specs,
        out_specs=pl.BlockSpec((1, 1, dv, tq), lambda b, h, i: (b, h, 0, i)),
        out_shape=jax.ShapeDtypeStruct((B, H, dv, Lq), BF16),
        scratch_shapes=[pltpu.VMEM((kc, tq), F32), pltpu.VMEM((kc, tq), F32),
                        pltpu.VMEM((1, tq), F32), pltpu.VMEM((1, tq), F32),
                        pltpu.VMEM((1, tq), F32), pltpu.VMEM((dvx, tq), F32)],
        compiler_params=pltpu.CompilerParams(
            dimension_semantics=("parallel", "parallel", "parallel"),
            vmem_limit_bytes=VMEM_LIMIT_BYTES),
        name="attn",
    )(*args)


def _merge_kernel(x_ref, mod_ref, ga_ref, ma_ref, u_ref, vg_ref, g_ref,
                  ws_ref, bs_ref, wb_ref, wo_ref, rw_ref, rb_ref,
                  xo_ref, tok_ref, lg_ref):
    tm = x_ref.shape[1]
    D = x_ref.shape[2]
    mod = mod_ref[0]

    lane = lax.broadcasted_iota(jnp.int32, (GMLP_CHUNK, LANES), 1)
    low_group = lane < GMLP_GROUP_DIM
    chunks = []
    for c in range(tm // GMLP_CHUNK):
        rows = slice(c * GMLP_CHUNK, (c + 1) * GMLP_CHUNK)
        cols = []
        for j in range(GMLP_WIDTH // LANES):
            vc = vg_ref[0, rows, j * LANES:(j + 1) * LANES]
            s_lo = jnp.dot(ws_ref[2 * j], vc, preferred_element_type=F32)
            s_hi = jnp.dot(ws_ref[2 * j + 1], vc, preferred_element_type=F32)
            cols.append(jnp.where(low_group, s_lo, s_hi))
        s = jnp.concatenate(cols, axis=1) + bs_ref[...]
        chunks.append((u_ref[0, rows, :].astype(F32) * s).astype(BF16))
    gm = chunks[0] if len(chunks) == 1 else jnp.concatenate(chunks, axis=0)

    branches = (ga_ref[0], ma_ref[0], gm)
    merged = None
    for i in range(N_BRANCHES):
        gate = jax.nn.sigmoid(g_ref[0, :, i * D:(i + 1) * D].astype(F32))
        term = gate * jnp.dot(branches[i], wb_ref[i], preferred_element_type=F32)
        merged = term if merged is None else merged + term
    y = jnp.dot(merged.astype(BF16), wo_ref[...], preferred_element_type=F32)
    x_new = x_ref[0] + mod[1:2] * _row_rms(y, mod[0:1])
    xo_ref[0] = x_new

    tok = (_row_rms(x_new, mod[2:3]) * mod[3:4] + mod[4:5]).astype(BF16)
    tok_ref[0] = tok
    lg_ref[0] = jnp.dot(tok, rw_ref[...], preferred_element_type=F32) + rb_ref[...]


def _merge_call(x, mod5, ga, ma, u, vg, g, ws, bs, wb, wo, rw, rb, tm):
    B, L, D = x.shape
    tok_spec = lambda w: pl.BlockSpec((1, tm, w), lambda b, i: (b, i, 0))
    consts = (ws, bs, wb, wo, rw, rb)
    return pl.pallas_call(
        _merge_kernel,
        grid=(B, L // tm),
        in_specs=[tok_spec(D), pl.BlockSpec((1,) + mod5.shape[1:], lambda b, i: (b, 0, 0)),
                  tok_spec(GQA_Q_WIDTH), tok_spec(MLA_V_WIDTH), tok_spec(GMLP_WIDTH),
                  tok_spec(GMLP_WIDTH), tok_spec(N_BRANCHES * D)] + [_full_spec(a) for a in consts],
        out_specs=[tok_spec(D), tok_spec(D), tok_spec(LANES)],
        out_shape=[jax.ShapeDtypeStruct((B, L, D), F32), jax.ShapeDtypeStruct((B, L, D), BF16),
                   jax.ShapeDtypeStruct((B, L, LANES), F32)],
        compiler_params=pltpu.CompilerParams(
            dimension_semantics=("parallel", "parallel"), vmem_limit_bytes=VMEM_LIMIT_BYTES),
        name="merge",
    )(x, mod5, ga, ma, u, vg, g, *consts)


def _expert_kernel(blk_expert_ref, n_used_ref, xb_ref, w1_ref, b1_ref, w2_ref, b2_ref, y_ref,
                   w1b_ref, w2b_ref):
    i = pl.program_id(0)
    d_expert = w2_ref.shape[1]
    new_expert = jnp.logical_or(i == 0, blk_expert_ref[i] != blk_expert_ref[jnp.maximum(i - 1, 0)])

    @pl.when(new_expert)
    def _():
        w1b_ref[...] = w1_ref[0].astype(BF16)
        w2b_ref[...] = w2_ref[0].astype(BF16)

    @pl.when(i < n_used_ref[0])
    def _():
        a = jnp.dot(xb_ref[...], w1b_ref[...], preferred_element_type=F32) + b1_ref[0]
        glu = jnp.minimum(a[:, :d_expert], SWIGLU_LIMIT)
        lin = jnp.clip(a[:, d_expert:], -SWIGLU_LIMIT, SWIGLU_LIMIT)
        act = glu * jax.nn.sigmoid(SWIGLU_ALPHA * glu) * (lin + 1.0)
        y_ref[...] = jnp.dot(act.astype(BF16), w2b_ref[...], preferred_element_type=F32) + b2_ref[0]

    @pl.when(i >= n_used_ref[0])
    def _():
        y_ref[...] = jnp.zeros(y_ref.shape, y_ref.dtype)


def _expert_call(blk_expert, n_used, xb, w1, b1, w2, b2, layer, rows_per_block):
    n_rows, D = xb.shape
    depth, E, _, two_de = w1.shape
    d_expert = w2.shape[2]
    grid_spec = pltpu.PrefetchScalarGridSpec(
        num_scalar_prefetch=2,
        grid=(n_rows // rows_per_block,),
        in_specs=[pl.BlockSpec((rows_per_block, D), lambda i, be, nu: (i, 0)),
                  pl.BlockSpec((None, 1, D, two_de), lambda i, be, nu: (layer, be[i], 0, 0)),
                  pl.BlockSpec((None, 1, 1, two_de), lambda i, be, nu: (layer, be[i], 0, 0)),
                  pl.BlockSpec((None, 1, d_expert, D), lambda i, be, nu: (layer, be[i], 0, 0)),
                  pl.BlockSpec((None, 1, 1, D), lambda i, be, nu: (layer, be[i], 0, 0))],
        out_specs=pl.BlockSpec((rows_per_block, D), lambda i, be, nu: (i, 0)),
        scratch_shapes=[pltpu.VMEM((D, two_de), BF16), pltpu.VMEM((d_expert, D), BF16)],
    )
    return pl.pallas_call(
        _expert_kernel,
        grid_spec=grid_spec,
        out_shape=jax.ShapeDtypeStruct((n_rows, D), F32),
        compiler_params=pltpu.CompilerParams(
            dimension_semantics=("arbitrary",), vmem_limit_bytes=VMEM_LIMIT_BYTES),
        name="experts",
    )(blk_expert, n_used, xb, w1, b1.reshape(depth, E, 1, two_de), w2, b2.reshape(depth, E, 1, D))


def _combine_kernel(dcur_ref, dnxt_ref, y_hbm, w_ref, x_ref, mod_ref, o_ref, gbuf, sem, *, n_tiles):
    s = pl.program_id(0)
    last = n_tiles - 1
    tm = x_ref.shape[0]
    slot = s % 2

    def row_copy(row, k, r, slot_):
        return pltpu.make_async_copy(y_hbm.at[pl.ds(row, 1)], gbuf.at[slot_, k, pl.ds(r, 1)],
                                     sem.at[slot_])

    def start_gather(d_ref, slot_):
        def token(r, carry):
            for k in range(TOP_K):
                row_copy(d_ref[0, 0, r * TOP_K + k], k, r, slot_).start()
            return carry
        lax.fori_loop(0, tm, token, 0)

    @pl.when(s == 0)
    def _():
        start_gather(dcur_ref, 0)

    @pl.when(s < last)
    def _():
        start_gather(dnxt_ref, 1 - slot)

    for k in range(TOP_K):
        pltpu.make_async_copy(y_hbm.at[pl.ds(0, tm)], gbuf.at[slot, k], sem.at[slot]).wait()

    w = w_ref[...]
    f = w[:, 0:1] * gbuf[slot, 0]
    for k in range(1, TOP_K):
        f = f + w[:, k:k + 1] * gbuf[slot, k]
    mod = mod_ref[0]
    o_ref[...] = x_ref[...] + mod[1:2] * _row_rms(f, mod[0:1])


def _combine_call(x, y, dest, top_w, mod2, tm):
    B, L, D = x.shape
    n_tiles = B * L // tm
    tiles_per_mod = n_tiles // mod2.shape[0]
    dest3 = dest.reshape(n_tiles, 1, tm * TOP_K)
    tok_spec = pl.BlockSpec((tm, D), lambda s: (s, 0))
    smem_spec = lambda idx: pl.BlockSpec((1, 1, tm * TOP_K), idx, memory_space=pltpu.SMEM)
    out = pl.pallas_call(
        functools.partial(_combine_kernel, n_tiles=n_tiles),
        grid=(n_tiles,),
        in_specs=[smem_spec(lambda s: (s, 0, 0)),
                  smem_spec(lambda s: (jnp.minimum(s + 1, n_tiles - 1), 0, 0)),
                  pl.BlockSpec(memory_space=pl.ANY),
                  pl.BlockSpec((tm, TOP_K), lambda s: (s, 0)),
                  tok_spec,
                  pl.BlockSpec((1,) + mod2.shape[1:], lambda s: (s // tiles_per_mod, 0, 0))],
        out_specs=tok_spec,
        out_shape=jax.ShapeDtypeStruct((B * L, D), F32),
        scratch_shapes=[pltpu.VMEM((2, TOP_K, tm, D), y.dtype), pltpu.SemaphoreType.DMA((2,))],
        compiler_params=pltpu.CompilerParams(
            dimension_semantics=("arbitrary",), vmem_limit_bytes=VMEM_LIMIT_BYTES),
        name="combine",
    )(dest3, dest3, y, top_w, x.reshape(B * L, D), mod2)
    return out.reshape(B, L, D)


def _rope_tables(n_lat, dtype=F32):
    rows = n_lat // GRID_W
    row = jnp.repeat(jnp.arange(rows, dtype=F32), GRID_W)
    col = jnp.tile(jnp.arange(GRID_W, dtype=F32), rows)

    def tables(rot_dim):
        quarter = rot_dim // 4
        inv_freq = ROPE_THETA ** (-jnp.arange(quarter, dtype=F32) / quarter)
        ang = jnp.concatenate([row[:, None] * inv_freq, col[:, None] * inv_freq], axis=-1)
        cos = jnp.concatenate([jnp.cos(ang), jnp.cos(ang)], axis=-1)
        sin = jnp.concatenate([-jnp.sin(ang), jnp.sin(ang)], axis=-1)
        return cos, sin

    cos64, sin64 = tables(GQA_HEAD_DIM)
    cos32, sin32 = tables(MLA_ROPE_DIM)
    tile = lambda t: jnp.tile(t, (1, LANES // t.shape[1]))
    pad_one = lambda t: jnp.concatenate([t, jnp.ones((n_lat, LANES - t.shape[1]), F32)], axis=1)
    pad_zero = lambda t: jnp.concatenate([t, jnp.zeros((n_lat, LANES - t.shape[1]), F32)], axis=1)
    return jnp.stack([tile(cos64), tile(sin64), tile(cos32), tile(sin32),
                      pad_one(cos32), pad_zero(sin32)]).astype(dtype)


def _identity_rope(n):
    one, zero = jnp.ones((n, LANES), F32), jnp.zeros((n, LANES), F32)
    return jnp.stack([one, zero, one, zero, one, zero])


def _group_mean_matrix(width, group):
    idx = np.arange(width) // group
    return jnp.asarray((idx[:, None] == idx[None, :]).astype(np.float32) / group, dtype=BF16)


def _layer_weights(w_in, mla_w_uq, mla_w_ukv, D):
    splits = np.cumsum((GQA_Q_WIDTH, GQA_KV_WIDTH, GQA_KV_WIDTH, MLA_Q_RANK, MLA_KV_RANK, MLA_ROPE_DIM,
                        2 * GMLP_WIDTH, N_BRANCHES * D))[:-1].tolist()
    wq, wk, wv, wcq, wckv, wkr, wuv, wg = jnp.split(w_in.astype(BF16), splits, axis=1)
    wkr = jnp.pad(wkr, ((0, 0), (0, LANES - MLA_ROPE_DIM)))
    uq = mla_w_uq.astype(BF16).reshape(MLA_Q_RANK, MLA_HEADS, MLA_NOPE_DIM + MLA_ROPE_DIM)
    wuq = jnp.concatenate([uq[:, :, :MLA_NOPE_DIM].reshape(MLA_Q_RANK, -1),
                           uq[:, :, MLA_NOPE_DIM:].reshape(MLA_Q_RANK, -1)], axis=1)
    ukv = mla_w_ukv.astype(BF16).reshape(MLA_KV_RANK, MLA_HEADS, MLA_NOPE_DIM + MLA_V_DIM)
    wukv = jnp.concatenate([ukv[:, :, :MLA_NOPE_DIM].reshape(MLA_KV_RANK, -1),
                            ukv[:, :, MLA_NOPE_DIM:].reshape(MLA_KV_RANK, -1)], axis=1)
    return (wq, wk, wv, wcq, wckv, wkr, wuv, wg, wuq, wukv,
            _group_mean_matrix(GQA_Q_WIDTH, GQA_HEAD_DIM), _group_mean_matrix(GQA_KV_WIDTH, GQA_HEAD_DIM))


def _small_params(qk_norm, mla_q_norm, mla_kv_norm, gmlp_v_norm):
    width = max(GQA_Q_WIDTH, GMLP_WIDTH)
    row = lambda v: jnp.pad(v, (0, width - v.shape[0]))
    rows = [row(jnp.tile(qk_norm[0], GQA_HEADS)), row(jnp.tile(qk_norm[1], GQA_KV_HEADS)),
            row(mla_q_norm), row(mla_kv_norm), row(gmlp_v_norm)]
    rows += [jnp.zeros((width,), F32)] * (8 - len(rows))
    return jnp.stack(rows).astype(F32)


def _stack_rows(rows, B, D):
    full = [jnp.broadcast_to(r, (B, D)) for r in rows]
    full += [jnp.zeros((B, D), F32)] * (8 - len(full))
    return jnp.stack(full, axis=1).astype(F32)


def _heads_last(ot, B, Lx):
    return jnp.transpose(ot, (0, 3, 1, 2)).reshape(B, Lx, ot.shape[1] * ot.shape[2])


def _attention_pair(queries, main, tail, tq):
    B, _, Lq, _ = queries["qg"].shape
    n, kc = main["vtg"].shape[2], main["vtg"].shape[4]
    group = GQA_HEADS // GQA_KV_HEADS
    chunked = lambda k: k.reshape(B, k.shape[1], n, kc, LANES)
    whole = lambda vt: vt.reshape(B, vt.shape[1], vt.shape[3], vt.shape[4])
    gqa_tail = (None, None) if tail is None else (tail["k"][:, None], whole(tail["vtg"]))
    mla_tail = (None, None) if tail is None else (tail["km"], whole(tail["vtm"]))
    gqa = _attn_call(queries["qg"], chunked(main["k"][:, None]), main["vtg"], *gqa_tail,
                     lambda h: 0, lambda h: h // group, tq)
    mla = _attn_call(queries["qm"], chunked(main["km"]), main["vtm"], *mla_tail,
                     lambda h: h, lambda h: h, tq)
    return _heads_last(gqa, B, Lq), _heads_last(mla, B, Lq)


def _route(logits, n_experts, rows_per_block):
    T = logits.shape[0]
    P = T * TOP_K
    top_logit, top_idx = lax.top_k(logits, TOP_K)
    top_w = jax.nn.softmax(top_logit, axis=-1)
    e_flat = top_idx.reshape(P)
    onehot = (e_flat[:, None] == jnp.arange(n_experts, dtype=e_flat.dtype)[None, :]).astype(jnp.int32)
    csum = jnp.cumsum(onehot, axis=0)
    rank = jnp.sum(onehot * (csum - 1), axis=1)
    counts = csum[-1]
    padded = (counts + rows_per_block - 1) // rows_per_block * rows_per_block
    pad_end = jnp.cumsum(padded)
    pad_start = pad_end - padded
    dest = pad_start[e_flat] + rank
    n_blocks = -(-(P + n_experts * (rows_per_block - 1)) // rows_per_block)
    blk_start = jnp.arange(n_blocks, dtype=jnp.int32) * rows_per_block
    blk_expert = jnp.minimum(jnp.sum((pad_end[None, :] <= blk_start[:, None]).astype(jnp.int32), axis=1),
                             n_experts - 1)
    n_used = (pad_end[-1] // rows_per_block).astype(jnp.int32).reshape(1)
    bits = max(1, (P - 1).bit_length())
    pair_sorted = jnp.sort((e_flat.astype(jnp.int32) << bits) | jnp.arange(P, dtype=jnp.int32)) & ((1 << bits) - 1)
    e_row = jnp.repeat(blk_expert, rows_per_block)
    within = jnp.arange(n_blocks * rows_per_block, dtype=jnp.int32) - pad_start[e_row]
    live = within < counts[e_row]
    grp_start = jnp.cumsum(counts) - counts
    src = jnp.where(live, grp_start[e_row] + within, 0)
    row_token = jnp.where(live, pair_sorted.at[src].get(mode="promise_in_bounds") // TOP_K, 0)
    return top_w, dest.reshape(T, TOP_K), row_token, blk_expert, n_used


def _layer(x, xc, mod, mod_c, rope_lat, rope_ctx, p, layer, update_ctx):
    B, L, D = x.shape
    C = xc.shape[1]
    (norms, w_in, qk_norm, mla_q_norm, mla_kv_norm, mla_w_uq, mla_w_ukv, gmlp_v_norm, gmlp_w_s,
     gmlp_b_s, w_branch, w_out, router_w, router_b, ew1, eb1, ew2, eb2) = p
    n_experts = router_w.shape[1]

    weights = _layer_weights(w_in, mla_w_uq, mla_w_ukv, D)
    small = _small_params(qk_norm, mla_q_norm, mla_kv_norm, gmlp_v_norm)
    sh1, sc1, g1, sh2, sc2, g2 = [mod[:, j] for j in range(N_MOD)]
    sh1c, sc1c, g1c, sh2c, sc2c, g2c = [mod_c[j] for j in range(N_MOD)]

    tm_lat = _pick_tile(L, (256, 128))
    tm_ctx = _pick_tile(C, (256, 128))
    kc_lat = _pick_tile(L, (ATTN_CHUNK, 512, 256, 128))
    assert C <= kc_lat, "context keys are absorbed as one tail chunk"
    lat = _proj_call(x, _stack_rows([norms[0], 1.0 + sc1, sh1], B, D), rope_lat, small, weights,
                     tm_lat, kc_lat)
    cx = _proj_call(xc, _stack_rows([norms[0], 1.0 + sc1c, sh1c], B, D), rope_ctx, small, weights,
                    tm_ctx, C)
    tq = _pick_tile(L, (512, 256, 128))
    gqa_o, mla_o = _attention_pair(lat, lat, cx, tq)
    u, vg, gates = lat["u"], lat["vg"], lat["gates"]

    ws = gmlp_w_s.astype(BF16)
    bs = jnp.repeat(gmlp_b_s.T, GMLP_GROUP_DIM, axis=1).astype(F32)
    wb = w_branch.astype(BF16)
    wo = w_out.astype(BF16)
    rw = jnp.pad(router_w, ((0, 0), (0, LANES - n_experts))).astype(BF16)
    rb = jnp.pad(router_b, (0, LANES - n_experts)).reshape(1, LANES).astype(F32)

    tm_merge = _pick_tile(L, (512, 256, 128))
    x1, tok, logits = _merge_call(x, _stack_rows([norms[1], g1, norms[2], 1.0 + sc2, sh2], B, D),
                                  gqa_o, mla_o, u, vg, gates, ws, bs, wb, wo, rw, rb, tm_merge)
    tok_all = tok.reshape(B * L, D)
    logits_all = logits.reshape(B * L, LANES)[:, :n_experts]
    if update_ctx:
        gqa_c, mla_c = _attention_pair(cx, cx, None, tm_ctx)
        xc1, tokc, logitsc = _merge_call(
            xc, _stack_rows([norms[1], g1c, norms[2], 1.0 + sc2c, sh2c], B, D),
            gqa_c, mla_c, cx["u"], cx["vg"], cx["gates"], ws, bs, wb, wo, rw, rb, tm_ctx)
        tok_all = jnp.concatenate([tok_all, tokc.reshape(B * C, D)], axis=0)
        logits_all = jnp.concatenate([logits_all, logitsc.reshape(B * C, LANES)[:, :n_experts]], axis=0)

    rows_per_block = 256
    top_w, dest, row_token, blk_expert, n_used = _route(logits_all, n_experts, rows_per_block)
    xb = tok_all.at[row_token].get(mode="promise_in_bounds")
    y = _expert_call(blk_expert, n_used, xb, ew1, eb1, ew2, eb2, layer, rows_per_block)

    x2 = _combine_call(x1, y, dest[:B * L], top_w[:B * L], _stack_rows([norms[3], g2], B, D), tm_merge)
    if update_ctx:
        xc = _combine_call(xc1, y, dest[B * L:], top_w[B * L:], _stack_rows([norms[3], g2c], B, D), tm_ctx)
    return x2, xc


def kernel(x, c, ctx, c_ctx, w_mod, b_mod, norm_gains, w_in, qk_norm, mla_q_norm, mla_kv_norm, mla_w_uq,
           mla_w_ukv, gmlp_v_norm, gmlp_w_s, gmlp_b_s, w_branch, w_out, router_w, router_b, expert_w1,
           expert_b1, expert_w2, expert_b2):
    B, L, D = x.shape
    C = ctx.shape[1]
    depth = w_mod.shape[0]
    rope_lat = _rope_tables(L)
    rope_ctx = _identity_rope(C)
    s_c = jax.nn.silu(c)
    s_cc = jax.nn.silu(c_ctx)
    xc = ctx
    for i in range(depth):
        mod = (jnp.dot(s_c, w_mod[i], precision=lax.Precision.HIGHEST) + b_mod[i]).reshape(B, N_MOD, D)
        mod_c = (jnp.dot(s_cc, w_mod[i], precision=lax.Precision.HIGHEST) + b_mod[i]).reshape(N_MOD, D)
        params = (norm_gains[i], w_in[i], qk_norm[i], mla_q_norm[i], mla_kv_norm[i], mla_w_uq[i],
                  mla_w_ukv[i], gmlp_v_norm[i], gmlp_w_s[i], gmlp_b_s[i], w_branch[i], w_out[i],
                  router_w[i], router_b[i], expert_w1, expert_b1, expert_w2, expert_b2)
        x, xc = _layer(x, xc, mod, mod_c, rope_lat, rope_ctx, params, i, update_ctx=(i < depth - 1))
    return x
```

```python
import functools
import math

import jax
import jax.numpy as jnp
import numpy as np
from jax import lax
from jax.experimental import pallas as pl
from jax.experimental.pallas import tpu as pltpu

GRID_W = 64
ROPE_THETA = 10000.0
RMS_EPS = 1e-6
N_MOD = 6
GQA_HEADS = 8
GQA_KV_HEADS = 2
GQA_HEAD_DIM = 64
MLA_HEADS = 8
MLA_NOPE_DIM = 64
MLA_ROPE_DIM = 32
MLA_V_DIM = 64
MLA_Q_RANK = 256
MLA_KV_RANK = 128
GMLP_GROUPS = 8
GMLP_GROUP_DIM = 64
GMLP_CHUNK = 128
N_BRANCHES = 3
TOP_K = 4
SWIGLU_ALPHA = 1.702
SWIGLU_LIMIT = 7.0

LANES = 128
VMEM_LIMIT_BYTES = 56 * 2**20
NEG_BIG = -1e30
ATTN_CHUNK = 1024
ATTN_LOOP_CHUNKS = 4
VT_EXTRA_ROWS = 16
EXPERT_BLOCK_ROWS = 512
LOG2E = math.log2(math.e)

BF16 = jnp.bfloat16
F32 = jnp.float32

GQA_Q_WIDTH = GQA_HEADS * GQA_HEAD_DIM
GQA_KV_WIDTH = GQA_KV_HEADS * GQA_HEAD_DIM
GMLP_WIDTH = GMLP_GROUPS * GMLP_GROUP_DIM
MLA_QN_WIDTH = MLA_HEADS * MLA_NOPE_DIM
MLA_QR_WIDTH = MLA_HEADS * MLA_ROPE_DIM
MLA_V_WIDTH = MLA_HEADS * MLA_V_DIM


def _pick_tile(n, candidates):
    for t in candidates:
        if n % t == 0:
            return t
    raise ValueError(f"no tile in {candidates} divides {n}")


def _full_spec(a):
    nd = a.ndim
    return pl.BlockSpec(a.shape, lambda *_: (0,) * nd)


def _row_rms(x, gain_row):
    ms = jnp.mean(x * x, axis=-1, keepdims=True)
    return x * lax.rsqrt(ms + RMS_EPS) * gain_row


def _group_rms(x, group_mean_mat, gain_row):
    ms = jnp.dot((x * x).astype(BF16), group_mean_mat, preferred_element_type=F32)
    return x * lax.rsqrt(ms + RMS_EPS) * gain_row


def _rope_lanes(x, cos_t, sin_t, half):
    rows, n = x.shape
    lane = lax.broadcasted_iota(jnp.int32, (rows, LANES), 1)
    first_half = (lane % (2 * half)) < half
    out = []
    for j in range(n // LANES):
        xb = x[:, j * LANES:(j + 1) * LANES]
        partner_up = pltpu.roll(xb, LANES - half, 1)
        partner_dn = pltpu.roll(xb, half, 1)
        partner = jnp.where(first_half, partner_up, partner_dn)
        out.append(xb * cos_t + partner * sin_t)
    return out[0] if len(out) == 1 else jnp.concatenate(out, axis=1)


def _proj_kernel(x_ref, mod_ref, rope_ref, small_ref,
                 wq_ref, wk_ref, wv_ref, wcq_ref, wckv_ref, wkr_ref, wuv_ref, wg_ref,
                 wuq_ref, wukv_ref, gq_ref, gk_ref,
                 qg_ref, k_ref, vtg_ref, qm_ref, km_ref, vtm_ref, u_ref, vg_ref, g_ref):
    tm = x_ref.shape[1]
    x = x_ref[0]
    mod = mod_ref[0]
    h = _row_rms(x, mod[0:1]) * mod[1:2] + mod[2:3]
    hb = h.astype(BF16)

    cos64, sin64 = rope_ref[0], rope_ref[1]
    cos32, sin32 = rope_ref[2], rope_ref[3]
    cos_kr, sin_kr = rope_ref[4], rope_ref[5]

    small = small_ref[...]
    qn_gain = small[0:1, :GQA_Q_WIDTH]
    kn_gain = small[1:2, :GQA_KV_WIDTH]
    cq_gain = small[2:3, :MLA_Q_RANK]
    ckv_gain = small[3:4, :MLA_KV_RANK]
    vg_gain = small[4:5, :GMLP_WIDTH]

    def proj(w_ref):
        return jnp.dot(hb, w_ref[...], preferred_element_type=F32)

    lane = lax.broadcasted_iota(jnp.int32, (tm, LANES), 1)
    half = LANES // 2
    block = lambda a, j: a[:, j * LANES:(j + 1) * LANES]
    to_lanes = lambda a, shift: a if shift % LANES == 0 else pltpu.roll(a, shift % LANES, 1)

    def store_values_t(v_block, dst_ref, first_head):
        vt = v_block.T
        extra_row = lax.broadcasted_iota(jnp.int32, (VT_EXTRA_ROWS, tm), 0)
        ones_then_zeros = jnp.where(extra_row == 0, 1.0, 0.0).astype(BF16)
        for i in range(2):
            dst_ref[0, first_head + i, 0, 0:half, :] = vt[i * half:(i + 1) * half].astype(BF16)
            dst_ref[0, first_head + i, 0, half:half + VT_EXTRA_ROWS, :] = ones_then_zeros

    q = _group_rms(proj(wq_ref), gq_ref[...], qn_gain)
    q = _rope_lanes(q, cos64, sin64, GQA_HEAD_DIM // 2) * (GQA_HEAD_DIM ** -0.5 * LOG2E)
    group = GQA_HEADS // GQA_KV_HEADS
    for hd in range(GQA_HEADS):
        src_half, dst_half = hd % 2, hd // group
        part = to_lanes(block(q, hd // 2), (dst_half - src_half) * half)
        keep = (lane >= half) if dst_half else (lane < half)
        qg_ref[0, hd] = jnp.where(keep, part, 0.0).astype(BF16)
    k = _group_rms(proj(wk_ref), gk_ref[...], kn_gain)
    k_ref[0] = _rope_lanes(k, cos64, sin64, GQA_HEAD_DIM // 2).astype(BF16)
    store_values_t(proj(wv_ref), vtg_ref, 0)

    cq = _row_rms(proj(wcq_ref), cq_gain).astype(BF16)
    qm = jnp.dot(cq, wuq_ref[...], preferred_element_type=F32)
    mla_scale = (MLA_NOPE_DIM + MLA_ROPE_DIM) ** -0.5 * LOG2E
    qn = qm[:, :MLA_QN_WIDTH] * mla_scale
    qr = _rope_lanes(qm[:, MLA_QN_WIDTH:], cos32, sin32, MLA_ROPE_DIM // 2) * mla_scale
    ckv = _row_rms(proj(wckv_ref), ckv_gain).astype(BF16)
    kv = jnp.dot(ckv, wukv_ref[...], preferred_element_type=F32)
    kn = kv[:, :MLA_QN_WIDTH]
    kr = _rope_lanes(proj(wkr_ref), cos_kr, sin_kr, MLA_ROPE_DIM // 2)
    kr_part = to_lanes(kr, MLA_NOPE_DIM)
    ropes_per_block = LANES // MLA_ROPE_DIM
    in_nope = lane < MLA_NOPE_DIM
    in_rope = lane < MLA_NOPE_DIM + MLA_ROPE_DIM
    for hd in range(MLA_HEADS):
        q_nope = to_lanes(block(qn, hd // 2), -(hd % 2) * half)
        q_rope = to_lanes(block(qr, hd // ropes_per_block),
                          MLA_NOPE_DIM - (hd % ropes_per_block) * MLA_ROPE_DIM)
        qm_ref[0, hd] = jnp.where(in_nope, q_nope, jnp.where(in_rope, q_rope, 0.0)).astype(BF16)
        k_nope = to_lanes(block(kn, hd // 2), -(hd % 2) * half)
        km_ref[0, hd] = jnp.where(in_nope, k_nope, jnp.where(in_rope, kr_part, 0.0)).astype(BF16)
    for j in range(MLA_V_WIDTH // LANES):
        store_values_t(block(kv, MLA_QN_WIDTH // LANES + j), vtm_ref, 2 * j)

    act = jax.nn.gelu(proj(wuv_ref), approximate=True)
    u_ref[0] = act[:, :GMLP_WIDTH].astype(BF16)
    vg_ref[0] = _row_rms(act[:, GMLP_WIDTH:], vg_gain).astype(BF16)

    g_ref[0] = proj(wg_ref).astype(BF16)


def _proj_call(x, mod3, rope, small, weights, tm, kc):
    B, L, D = x.shape
    per_chunk = kc // tm
    dvx = GQA_HEAD_DIM + VT_EXTRA_ROWS
    tok_spec = lambda w: pl.BlockSpec((1, tm, w), lambda b, i: (b, i, 0))
    slab_spec = lambda H: pl.BlockSpec((1, H, tm, LANES), lambda b, i: (b, 0, i, 0))
    vt_spec = lambda H: pl.BlockSpec((1, H, 1, dvx, tm),
                                     lambda b, i: (b, 0, i // per_chunk, 0, i % per_chunk))
    tok_shape = lambda w: jax.ShapeDtypeStruct((B, L, w), BF16)
    slab_shape = lambda H: jax.ShapeDtypeStruct((B, H, L, LANES), BF16)
    vt_shape = lambda H: jax.ShapeDtypeStruct((B, H, L // kc, dvx, kc), BF16)
    in_specs = [tok_spec(D),
                pl.BlockSpec((1,) + mod3.shape[1:], lambda b, i: (b, 0, 0)),
                pl.BlockSpec((rope.shape[0], tm, LANES), lambda b, i: (0, i, 0)),
                _full_spec(small)] + [_full_spec(w) for w in weights]
    names = ("qg", "k", "vtg", "qm", "km", "vtm", "u", "vg", "gates")
    out_specs = [slab_spec(GQA_HEADS), tok_spec(GQA_KV_WIDTH), vt_spec(GQA_KV_HEADS),
                 slab_spec(MLA_HEADS), slab_spec(MLA_HEADS), vt_spec(MLA_HEADS),
                 tok_spec(GMLP_WIDTH), tok_spec(GMLP_WIDTH), tok_spec(N_BRANCHES * D)]
    out_shape = [slab_shape(GQA_HEADS), tok_shape(GQA_KV_WIDTH), vt_shape(GQA_KV_HEADS),
                 slab_shape(MLA_HEADS), slab_shape(MLA_HEADS), vt_shape(MLA_HEADS),
                 tok_shape(GMLP_WIDTH), tok_shape(GMLP_WIDTH), tok_shape(N_BRANCHES * D)]
    outs = pl.pallas_call(
        _proj_kernel,
        grid=(B, L // tm),
        in_specs=in_specs,
        out_specs=out_specs,
        out_shape=out_shape,
        compiler_params=pltpu.CompilerParams(
            dimension_semantics=("parallel", "parallel"), vmem_limit_bytes=VMEM_LIMIT_BYTES),
        name="proj",
    )(x, mod3, rope, small, *weights)
    return dict(zip(names, outs))


def _attn_kernel(*refs, n_main, has_tail):
    if has_tail:
        (q_ref, k_ref, vt_ref, kt_ref, vtt_ref, o_ref,
         sa_ref, sb_ref, ca_ref, cb_ref, m_ref, acc_ref) = refs
    else:
        q_ref, k_ref, vt_ref, o_ref, sa_ref, sb_ref, ca_ref, cb_ref, m_ref, acc_ref = refs
        kt_ref = vtt_ref = None
    bufs = ((sa_ref, ca_ref), (sb_ref, cb_ref))
    dv = o_ref.shape[2]
    q_t = q_ref[0, 0].astype(F32).T.astype(BF16)

    m_ref[...] = jnp.full(m_ref.shape, NEG_BIG, F32)
    acc_ref[...] = jnp.zeros(acc_ref.shape, F32)

    def scores(k_chunk, dst):
        st_ref, cmax_ref = dst
        st = jnp.dot(k_chunk, q_t, preferred_element_type=F32)
        st_ref[0:k_chunk.shape[0], :] = st
        cmax_ref[...] = jnp.max(st, axis=0, keepdims=True)

    def absorb(src, vt_chunk):
        st_ref, cmax_ref = src
        m = m_ref[...]
        m_new = jnp.maximum(m, cmax_ref[...])
        alpha = jnp.exp2(m - m_new)
        pt = jnp.exp2(st_ref[0:vt_chunk.shape[1], :] - m_new).astype(BF16)
        acc_ref[...] = alpha * acc_ref[...] + jnp.dot(vt_chunk, pt, preferred_element_type=F32)
        m_ref[...] = m_new

    main_k = lambda c: k_ref[0, 0, c]
    main_vt = lambda c: vt_ref[0, 0, c]

    scores(main_k(0), bufs[0])
    group = ATTN_LOOP_CHUNKS
    n_groups = (n_main - 1) // group
    if n_groups:
        def run_group(g, carry):
            for j in range(group):
                c = group * g + j
                scores(main_k(c + 1), bufs[(j + 1) % 2])
                absorb(bufs[j % 2], main_vt(c))
            return carry
        lax.fori_loop(0, n_groups, run_group, 0)
    rest = [(main_k, main_vt, c) for c in range(group * n_groups, n_main)]
    if has_tail:
        rest.append((lambda _: kt_ref[0, 0], lambda _: vtt_ref[0, 0], 0))
    for j, (k_of, vt_of, c) in enumerate(rest):
        if j + 1 < len(rest):
            nk_of, _, nc = rest[j + 1]
            scores(nk_of(nc), bufs[(j + 1) % 2])
        absorb(bufs[j % 2], vt_of(c))

    acc = acc_ref[...]
    o_ref[0, 0] = (acc[0:dv] / acc[dv:dv + 1]).astype(o_ref.dtype)


def _attn_call(q, k, vt, k_tail, vt_tail, k_head_of, v_head_of, tq):
    B, H, Lq, _ = q.shape
    _, _, n_main, kc, _ = k.shape
    dvx = vt.shape[3]
    dv = dvx - VT_EXTRA_ROWS
    has_tail = k_tail is not None
    in_specs = [pl.BlockSpec((1, 1, tq, LANES), lambda b, h, i: (b, h, i, 0)),
                pl.BlockSpec((1, 1, n_main, kc, LANES), lambda b, h, i: (b, k_head_of(h), 0, 0, 0)),
                pl.BlockSpec((1, 1, n_main, dvx, kc), lambda b, h, i: (b, v_head_of(h), 0, 0, 0))]
    args = [q, k, vt]
    if has_tail:
        kt = k_tail.shape[2]
        assert kt <= kc
        in_specs += [pl.BlockSpec((1, 1, kt, LANES), lambda b, h, i: (b, k_head_of(h), 0, 0)),
                     pl.BlockSpec((1, 1, dvx, kt), lambda b, h, i: (b, v_head_of(h), 0, 0))]
        args += [k_tail, vt_tail]
    return pl.pallas_call(
        functools.partial(_attn_kernel, n_main=n_main, has_tail=has_tail),
        grid=(B, H, Lq // tq),
        in_specs=in_specs,
        out_specs=pl.BlockSpec((1, 1, dv, tq), lambda b, h, i: (b, h, 0, i)),
        out_shape=jax.ShapeDtypeStruct((B, H, dv, Lq), BF16),
        scratch_shapes=[pltpu.VMEM((kc, tq), F32), pltpu.VMEM((kc, tq), F32),
                        pltpu.VMEM((1, tq), F32), pltpu.VMEM((1, tq), F32),
                        pltpu.VMEM((1, tq), F32), pltpu.VMEM((dvx, tq), F32)],
        compiler_params=pltpu.CompilerParams(
            dimension_semantics=("parallel", "parallel", "parallel"),
            vmem_limit_bytes=VMEM_LIMIT_BYTES),
        name="attn",
    )(*args)


def _merge_kernel(x_ref, mod_ref, ga_ref, ma_ref, u_ref, vg_ref, g_ref,
                  ws_ref, bs_ref, wb_ref, wo_ref, rw_ref, rb_ref,
                  xo_ref, tok_ref, lg_ref):
    tm = x_ref.shape[1]
    D = x_ref.shape[2]
    mod = mod_ref[0]

    lane = lax.broadcasted_iota(jnp.int32, (GMLP_CHUNK, LANES), 1)
    low_group = lane < GMLP_GROUP_DIM
    chunks = []
    for c in range(tm // GMLP_CHUNK):
        rows = slice(c * GMLP_CHUNK, (c + 1) * GMLP_CHUNK)
        cols = []
        for j in range(GMLP_WIDTH // LANES):
            vc = vg_ref[0, rows, j * LANES:(j + 1) * LANES]
            s_lo = jnp.dot(ws_ref[2 * j], vc, preferred_element_type=F32)
            s_hi = jnp.dot(ws_ref[2 * j + 1], vc, preferred_element_type=F32)
            cols.append(jnp.where(low_group, s_lo, s_hi))
        s = jnp.concatenate(cols, axis=1) + bs_ref[...]
        chunks.append((u_ref[0, rows, :].astype(F32) * s).astype(BF16))
    gm = chunks[0] if len(chunks) == 1 else jnp.concatenate(chunks, axis=0)

    branches = (ga_ref[0], ma_ref[0], gm)
    merged = None
    for i in range(N_BRANCHES):
        gate = jax.nn.sigmoid(g_ref[0, :, i * D:(i + 1) * D].astype(F32))
        term = gate * jnp.dot(branches[i], wb_ref[i], preferred_element_type=F32)
        merged = term if merged is None else merged + term
    y = jnp.dot(merged.astype(BF16), wo_ref[...], preferred_element_type=F32)
    x_new = x_ref[0] + mod[1:2] * _row_rms(y, mod[0:1])
    xo_ref[0] = x_new

    tok = (_row_rms(x_new, mod[2:3]) * mod[3:4] + mod[4:5]).astype(BF16)
    tok_ref[0] = tok
    lg_ref[0] = jnp.dot(tok, rw_ref[...], preferred_element_type=F32) + rb_ref[...]


def _merge_call(x, mod5, ga, ma, u, vg, g, ws, bs, wb, wo, rw, rb, tm):
    B, L, D = x.shape
    tok_spec = lambda w: pl.BlockSpec((1, tm, w), lambda b, i: (b, i, 0))
    consts = (ws, bs, wb, wo, rw, rb)
    return pl.pallas_call(
        _merge_kernel,
        grid=(B, L // tm),
        in_specs=[tok_spec(D), pl.BlockSpec((1,) + mod5.shape[1:], lambda b, i: (b, 0, 0)),
                  tok_spec(GQA_Q_WIDTH), tok_spec(MLA_V_WIDTH), tok_spec(GMLP_WIDTH),
                  tok_spec(GMLP_WIDTH), tok_spec(N_BRANCHES * D)] + [_full_spec(a) for a in consts],
        out_specs=[tok_spec(D), tok_spec(D), tok_spec(LANES)],
        out_shape=[jax.ShapeDtypeStruct((B, L, D), F32), jax.ShapeDtypeStruct((B, L, D), BF16),
                   jax.ShapeDtypeStruct((B, L, LANES), F32)],
        compiler_params=pltpu.CompilerParams(
            dimension_semantics=("parallel", "parallel"), vmem_limit_bytes=VMEM_LIMIT_BYTES),
        name="merge",
    )(x, mod5, ga, ma, u, vg, g, *consts)


def _expert_kernel(blk_expert_ref, n_used_ref, xb_ref, w1_ref, b1_ref, w2_ref, b2_ref, y_ref,
                   w1b_ref, w2b_ref):
    i = pl.program_id(0)
    d_expert = w2_ref.shape[1]
    new_expert = jnp.logical_or(i == 0, blk_expert_ref[i] != blk_expert_ref[jnp.maximum(i - 1, 0)])

    @pl.when(new_expert)
    def _():
        w1b_ref[...] = w1_ref[0].astype(BF16)
        w2b_ref[...] = w2_ref[0].astype(BF16)

    @pl.when(i < n_used_ref[0])
    def _():
        a = jnp.dot(xb_ref[...], w1b_ref[...], preferred_element_type=F32) + b1_ref[0]
        glu = jnp.minimum(a[:, :d_expert], SWIGLU_LIMIT)
        lin = jnp.clip(a[:, d_expert:], -SWIGLU_LIMIT, SWIGLU_LIMIT)
        act = glu * jax.nn.sigmoid(SWIGLU_ALPHA * glu) * (lin + 1.0)
        y_ref[...] = jnp.dot(act.astype(BF16), w2b_ref[...], preferred_element_type=F32) + b2_ref[0]

    @pl.when(i >= n_used_ref[0])
    def _():
        y_ref[...] = jnp.zeros(y_ref.shape, y_ref.dtype)


def _expert_call(blk_expert, n_used, xb, w1, b1, w2, b2, layer, rows_per_block):
    n_rows, D = xb.shape
    depth, E, _, two_de = w1.shape
    d_expert = w2.shape[2]
    grid_spec = pltpu.PrefetchScalarGridSpec(
        num_scalar_prefetch=2,
        grid=(n_rows // rows_per_block,),
        in_specs=[pl.BlockSpec((rows_per_block, D), lambda i, be, nu: (i, 0)),
                  pl.BlockSpec((None, 1, D, two_de), lambda i, be, nu: (layer, be[i], 0, 0)),
                  pl.BlockSpec((None, 1, 1, two_de), lambda i, be, nu: (layer, be[i], 0, 0)),
                  pl.BlockSpec((None, 1, d_expert, D), lambda i, be, nu: (layer, be[i], 0, 0)),
                  pl.BlockSpec((None, 1, 1, D), lambda i, be, nu: (layer, be[i], 0, 0))],
        out_specs=pl.BlockSpec((rows_per_block, D), lambda i, be, nu: (i, 0)),
        scratch_shapes=[pltpu.VMEM((D, two_de), BF16), pltpu.VMEM((d_expert, D), BF16)],
    )
    return pl.pallas_call(
        _expert_kernel,
        grid_spec=grid_spec,
        out_shape=jax.ShapeDtypeStruct((n_rows, D), F32),
        compiler_params=pltpu.CompilerParams(
            dimension_semantics=("arbitrary",), vmem_limit_bytes=VMEM_LIMIT_BYTES),
        name="experts",
    )(blk_expert, n_used, xb, w1, b1.reshape(depth, E, 1, two_de), w2, b2.reshape(depth, E, 1, D))


def _combine_kernel(dcur_ref, dnxt_ref, y_hbm, w_ref, x_ref, mod_ref, o_ref, gbuf, sem, *, n_tiles):
    s = pl.program_id(0)
    last = n_tiles - 1
    tm = x_ref.shape[0]
    slot = s % 2

    def row_copy(row, k, r, slot_):
        return pltpu.make_async_copy(y_hbm.at[pl.ds(row, 1)], gbuf.at[slot_, k, pl.ds(r, 1)],
                                     sem.at[slot_])

    def start_gather(d_ref, slot_):
        def token(r, carry):
            for k in range(TOP_K):
                row_copy(d_ref[0, 0, r * TOP_K + k], k, r, slot_).start()
            return carry
        lax.fori_loop(0, tm, token, 0, unroll=8)

    @pl.when(s == 0)
    def _():
        start_gather(dcur_ref, 0)

    @pl.when(s < last)
    def _():
        start_gather(dnxt_ref, 1 - slot)

    for k in range(TOP_K):
        pltpu.make_async_copy(y_hbm.at[pl.ds(0, tm)], gbuf.at[slot, k], sem.at[slot]).wait()

    w = w_ref[...]
    f = w[:, 0:1] * gbuf[slot, 0]
    for k in range(1, TOP_K):
        f = f + w[:, k:k + 1] * gbuf[slot, k]
    mod = mod_ref[0]
    o_ref[...] = x_ref[...] + mod[1:2] * _row_rms(f, mod[0:1])


def _combine_call(x, y, dest, top_w, mod2, tm):
    B, L, D = x.shape
    n_tiles = B * L // tm
    tiles_per_mod = n_tiles // mod2.shape[0]
    dest3 = dest.reshape(n_tiles, 1, tm * TOP_K)
    tok_spec = pl.BlockSpec((tm, D), lambda s: (s, 0))
    smem_spec = lambda idx: pl.BlockSpec((1, 1, tm * TOP_K), idx, memory_space=pltpu.SMEM)
    out = pl.pallas_call(
        functools.partial(_combine_kernel, n_tiles=n_tiles),
        grid=(n_tiles,),
        in_specs=[smem_spec(lambda s: (s, 0, 0)),
                  smem_spec(lambda s: (jnp.minimum(s + 1, n_tiles - 1), 0, 0)),
                  pl.BlockSpec(memory_space=pl.ANY),
                  pl.BlockSpec((tm, TOP_K), lambda s: (s, 0)),
                  tok_spec,
                  pl.BlockSpec((1,) + mod2.shape[1:], lambda s: (s // tiles_per_mod, 0, 0))],
        out_specs=tok_spec,
        out_shape=jax.ShapeDtypeStruct((B * L, D), F32),
        scratch_shapes=[pltpu.VMEM((2, TOP_K, tm, D), y.dtype), pltpu.SemaphoreType.DMA((2,))],
        compiler_params=pltpu.CompilerParams(
            dimension_semantics=("arbitrary",), vmem_limit_bytes=VMEM_LIMIT_BYTES),
        name="combine",
    )(dest3, dest3, y, top_w, x.reshape(B * L, D), mod2)
    return out.reshape(B, L, D)


def _rope_tables(n_lat, dtype=F32):
    rows = n_lat // GRID_W
    row = jnp.repeat(jnp.arange(rows, dtype=F32), GRID_W)
    col = jnp.tile(jnp.arange(GRID_W, dtype=F32), rows)

    def tables(rot_dim):
        quarter = rot_dim // 4
        inv_freq = ROPE_THETA ** (-jnp.arange(quarter, dtype=F32) / quarter)
        ang = jnp.concatenate([row[:, None] * inv_freq, col[:, None] * inv_freq], axis=-1)
        cos = jnp.concatenate([jnp.cos(ang), jnp.cos(ang)], axis=-1)
        sin = jnp.concatenate([-jnp.sin(ang), jnp.sin(ang)], axis=-1)
        return cos, sin

    cos64, sin64 = tables(GQA_HEAD_DIM)
    cos32, sin32 = tables(MLA_ROPE_DIM)
    tile = lambda t: jnp.tile(t, (1, LANES // t.shape[1]))
    pad_one = lambda t: jnp.concatenate([t, jnp.ones((n_lat, LANES - t.shape[1]), F32)], axis=1)
    pad_zero = lambda t: jnp.concatenate([t, jnp.zeros((n_lat, LANES - t.shape[1]), F32)], axis=1)
    return jnp.stack([tile(cos64), tile(sin64), tile(cos32), tile(sin32),
                      pad_one(cos32), pad_zero(sin32)]).astype(dtype)


def _identity_rope(n):
    one, zero = jnp.ones((n, LANES), F32), jnp.zeros((n, LANES), F32)
    return jnp.stack([one, zero, one, zero, one, zero])


def _group_mean_matrix(width, group):
    idx = np.arange(width) // group
    return jnp.asarray((idx[:, None] == idx[None, :]).astype(np.float32) / group, dtype=BF16)


def _layer_weights(w_in, mla_w_uq, mla_w_ukv, D):
    splits = np.cumsum((GQA_Q_WIDTH, GQA_KV_WIDTH, GQA_KV_WIDTH, MLA_Q_RANK, MLA_KV_RANK, MLA_ROPE_DIM,
                        2 * GMLP_WIDTH, N_BRANCHES * D))[:-1].tolist()
    wq, wk, wv, wcq, wckv, wkr, wuv, wg = jnp.split(w_in.astype(BF16), splits, axis=1)
    wkr = jnp.pad(wkr, ((0, 0), (0, LANES - MLA_ROPE_DIM)))
    uq = mla_w_uq.astype(BF16).reshape(MLA_Q_RANK, MLA_HEADS, MLA_NOPE_DIM + MLA_ROPE_DIM)
    wuq = jnp.concatenate([uq[:, :, :MLA_NOPE_DIM].reshape(MLA_Q_RANK, -1),
                           uq[:, :, MLA_NOPE_DIM:].reshape(MLA_Q_RANK, -1)], axis=1)
    ukv = mla_w_ukv.astype(BF16).reshape(MLA_KV_RANK, MLA_HEADS, MLA_NOPE_DIM + MLA_V_DIM)
    wukv = jnp.concatenate([ukv[:, :, :MLA_NOPE_DIM].reshape(MLA_KV_RANK, -1),
                            ukv[:, :, MLA_NOPE_DIM:].reshape(MLA_KV_RANK, -1)], axis=1)
    return (wq, wk, wv, wcq, wckv, wkr, wuv, wg, wuq, wukv,
            _group_mean_matrix(GQA_Q_WIDTH, GQA_HEAD_DIM), _group_mean_matrix(GQA_KV_WIDTH, GQA_HEAD_DIM))


def _small_params(qk_norm, mla_q_norm, mla_kv_norm, gmlp_v_norm):
    width = max(GQA_Q_WIDTH, GMLP_WIDTH)
    row = lambda v: jnp.pad(v, (0, width - v.shape[0]))
    rows = [row(jnp.tile(qk_norm[0], GQA_HEADS)), row(jnp.tile(qk_norm[1], GQA_KV_HEADS)),
            row(mla_q_norm), row(mla_kv_norm), row(gmlp_v_norm)]
    rows += [jnp.zeros((width,), F32)] * (8 - len(rows))
    return jnp.stack(rows).astype(F32)


def _stack_rows(rows, B, D):
    full = [jnp.broadcast_to(r, (B, D)) for r in rows]
    full += [jnp.zeros((B, D), F32)] * (8 - len(full))
    return jnp.stack(full, axis=1).astype(F32)


def _heads_last(ot, B, Lx):
    return jnp.transpose(ot, (0, 3, 1, 2)).reshape(B, Lx, ot.shape[1] * ot.shape[2])


def _attention_pair(queries, main, tail, tq):
    B, _, Lq, _ = queries["qg"].shape
    n, kc = main["vtg"].shape[2], main["vtg"].shape[4]
    group = GQA_HEADS // GQA_KV_HEADS
    chunked = lambda k: k.reshape(B, k.shape[1], n, kc, LANES)
    whole = lambda vt: vt.reshape(B, vt.shape[1], vt.shape[3], vt.shape[4])
    gqa_tail = (None, None) if tail is None else (tail["k"][:, None], whole(tail["vtg"]))
    mla_tail = (None, None) if tail is None else (tail["km"], whole(tail["vtm"]))
    gqa = _attn_call(queries["qg"], chunked(main["k"][:, None]), main["vtg"], *gqa_tail,
                     lambda h: 0, lambda h: h // group, tq)
    mla = _attn_call(queries["qm"], chunked(main["km"]), main["vtm"], *mla_tail,
                     lambda h: h, lambda h: h, tq)
    return _heads_last(gqa, B, Lq), _heads_last(mla, B, Lq)


def _route(logits, n_experts, rows_per_block):
    T = logits.shape[0]
    P = T * TOP_K
    top_logit, top_idx = lax.top_k(logits, TOP_K)
    top_w = jax.nn.softmax(top_logit, axis=-1)
    e_flat = top_idx.reshape(P)
    onehot = (e_flat[:, None] == jnp.arange(n_experts, dtype=e_flat.dtype)[None, :]).astype(jnp.int32)
    csum = jnp.cumsum(onehot, axis=0)
    rank = jnp.sum(onehot * (csum - 1), axis=1)
    counts = csum[-1]
    padded = (counts + rows_per_block - 1) // rows_per_block * rows_per_block
    pad_end = jnp.cumsum(padded)
    pad_start = pad_end - padded
    dest = pad_start[e_flat] + rank
    n_blocks = -(-(P + n_experts * (rows_per_block - 1)) // rows_per_block)
    blk_start = jnp.arange(n_blocks, dtype=jnp.int32) * rows_per_block
    blk_expert = jnp.minimum(jnp.sum((pad_end[None, :] <= blk_start[:, None]).astype(jnp.int32), axis=1),
                             n_experts - 1)
    n_used = (pad_end[-1] // rows_per_block).astype(jnp.int32).reshape(1)
    bits = max(1, (P - 1).bit_length())
    pair_sorted = jnp.sort((e_flat.astype(jnp.int32) << bits) | jnp.arange(P, dtype=jnp.int32)) & ((1 << bits) - 1)
    e_row = jnp.repeat(blk_expert, rows_per_block)
    within = jnp.arange(n_blocks * rows_per_block, dtype=jnp.int32) - pad_start[e_row]
    live = within < counts[e_row]
    grp_start = jnp.cumsum(counts) - counts
    src = jnp.where(live, grp_start[e_row] + within, 0)
    row_token = jnp.where(live, pair_sorted.at[src].get(mode="promise_in_bounds") // TOP_K, 0)
    return top_w, dest.reshape(T, TOP_K), row_token, blk_expert, n_used


def _layer(x, xc, mod, mod_c, rope_lat, rope_ctx, p, layer, update_ctx):
    B, L, D = x.shape
    C = xc.shape[1]
    (norms, w_in, qk_norm, mla_q_norm, mla_kv_norm, mla_w_uq, mla_w_ukv, gmlp_v_norm, gmlp_w_s,
     gmlp_b_s, w_branch, w_out, router_w, router_b, ew1, eb1, ew2, eb2) = p
    n_experts = router_w.shape[1]

    weights = _layer_weights(w_in, mla_w_uq, mla_w_ukv, D)
    small = _small_params(qk_norm, mla_q_norm, mla_kv_norm, gmlp_v_norm)
    sh1, sc1, g1, sh2, sc2, g2 = [mod[:, j] for j in range(N_MOD)]
    sh1c, sc1c, g1c, sh2c, sc2c, g2c = [mod_c[j] for j in range(N_MOD)]

    tm_lat = _pick_tile(L, (256, 128))
    tm_ctx = _pick_tile(C, (256, 128))
    kc_lat = _pick_tile(L, (ATTN_CHUNK, 512, 256, 128))
    assert C <= kc_lat, "context keys are absorbed as one tail chunk"
    lat = _proj_call(x, _stack_rows([norms[0], 1.0 + sc1, sh1], B, D), rope_lat, small, weights,
                     tm_lat, kc_lat)
    cx = _proj_call(xc, _stack_rows([norms[0], 1.0 + sc1c, sh1c], B, D), rope_ctx, small, weights,
                    tm_ctx, C)
    tq = _pick_tile(L, (512, 256, 128))
    gqa_o, mla_o = _attention_pair(lat, lat, cx, tq)
    u, vg, gates = lat["u"], lat["vg"], lat["gates"]

    ws = gmlp_w_s.astype(BF16)
    bs = jnp.repeat(gmlp_b_s.T, GMLP_GROUP_DIM, axis=1).astype(F32)
    wb = w_branch.astype(BF16)
    wo = w_out.astype(BF16)
    rw = jnp.pad(router_w, ((0, 0), (0, LANES - n_experts))).astype(BF16)
    rb = jnp.pad(router_b, (0, LANES - n_experts)).reshape(1, LANES).astype(F32)

    tm_merge = _pick_tile(L, (512, 256, 128))
    x1, tok, logits = _merge_call(x, _stack_rows([norms[1], g1, norms[2], 1.0 + sc2, sh2], B, D),
                                  gqa_o, mla_o, u, vg, gates, ws, bs, wb, wo, rw, rb, tm_merge)
    tok_all = tok.reshape(B * L, D)
    logits_all = logits.reshape(B * L, LANES)[:, :n_experts]
    if update_ctx:
        gqa_c, mla_c = _attention_pair(cx, cx, None, tm_ctx)
        xc1, tokc, logitsc = _merge_call(
            xc, _stack_rows([norms[1], g1c, norms[2], 1.0 + sc2c, sh2c], B, D),
            gqa_c, mla_c, cx["u"], cx["vg"], cx["gates"], ws, bs, wb, wo, rw, rb, tm_ctx)
        tok_all = jnp.concatenate([tok_all, tokc.reshape(B * C, D)], axis=0)
        logits_all = jnp.concatenate([logits_all, logitsc.reshape(B * C, LANES)[:, :n_experts]], axis=0)

    rows_per_block = EXPERT_BLOCK_ROWS
    top_w, dest, row_token, blk_expert, n_used = _route(logits_all, n_experts, rows_per_block)
    xb = tok_all.at[row_token].get(mode="promise_in_bounds")
    y = _expert_call(blk_expert, n_used, xb, ew1, eb1, ew2, eb2, layer, rows_per_block)

    x2 = _combine_call(x1, y, dest[:B * L], top_w[:B * L], _stack_rows([norms[3], g2], B, D), tm_merge)
    if update_ctx:
        xc = _combine_call(xc1, y, dest[B * L:], top_w[B * L:], _stack_rows([norms[3], g2c], B, D), tm_ctx)
    return x2, xc


def kernel(x, c, ctx, c_ctx, w_mod, b_mod, norm_gains, w_in, qk_norm, mla_q_norm, mla_kv_norm, mla_w_uq,
           mla_w_ukv, gmlp_v_norm, gmlp_w_s, gmlp_b_s, w_branch, w_out, router_w, router_b, expert_w1,
           expert_b1, expert_w2, expert_b2):
    B, L, D = x.shape
    C = ctx.shape[1]
    depth = w_mod.shape[0]
    rope_lat = _rope_tables(L)
    rope_ctx = _identity_rope(C)
    s_c = jax.nn.silu(c)
    s_cc = jax.nn.silu(c_ctx)
    xc = ctx
    for i in range(depth):
        mod = (jnp.dot(s_c, w_mod[i], precision=lax.Precision.HIGHEST) + b_mod[i]).reshape(B, N_MOD, D)
        mod_c = (jnp.dot(s_cc, w_mod[i], precision=lax.Precision.HIGHEST) + b_mod[i]).reshape(N_MOD, D)
        params = (norm_gains[i], w_in[i], qk_norm[i], mla_q_norm[i], mla_kv_norm[i], mla_w_uq[i],
                  mla_w_ukv[i], gmlp_v_norm[i], gmlp_w_s[i], gmlp_b_s[i], w_branch[i], w_out[i],
                  router_w[i], router_b[i], expert_w1, expert_b1, expert_w2, expert_b2)
        x, xc = _layer(x, xc, mod, mod_c, rope_lat, rope_ctx, params, i, update_ctx=(i < depth - 1))
    return x
```

```python
import functools
import math

import jax
import jax.numpy as jnp
import numpy as np
from jax import lax
from jax.experimental import pallas as pl
from jax.experimental.pallas import tpu as pltpu

GRID_W = 64
ROPE_THETA = 10000.0
RMS_EPS = 1e-6
N_MOD = 6
GQA_HEADS = 8
GQA_KV_HEADS = 2
GQA_HEAD_DIM = 64
MLA_HEADS = 8
MLA_NOPE_DIM = 64
MLA_ROPE_DIM = 32
MLA_V_DIM = 64
MLA_Q_RANK = 256
MLA_KV_RANK = 128
GMLP_GROUPS = 8
GMLP_GROUP_DIM = 64
GMLP_CHUNK = 128
N_BRANCHES = 3
TOP_K = 4
SWIGLU_ALPHA = 1.702
SWIGLU_LIMIT = 7.0

LANES = 128
VMEM_LIMIT_BYTES = 56 * 2**20
NEG_BIG = -1e30
ATTN_CHUNK = 1024
ATTN_LOOP_CHUNKS = 4
VT_EXTRA_ROWS = 16
EXPERT_BLOCK_ROWS = 512
LOG2E = math.log2(math.e)

BF16 = jnp.bfloat16
F32 = jnp.float32

GQA_Q_WIDTH = GQA_HEADS * GQA_HEAD_DIM
GQA_KV_WIDTH = GQA_KV_HEADS * GQA_HEAD_DIM
GMLP_WIDTH = GMLP_GROUPS * GMLP_GROUP_DIM
MLA_QN_WIDTH = MLA_HEADS * MLA_NOPE_DIM
MLA_QR_WIDTH = MLA_HEADS * MLA_ROPE_DIM
MLA_V_WIDTH = MLA_HEADS * MLA_V_DIM


def _pick_tile(n, candidates):
    for t in candidates:
        if n % t == 0:
            return t
    raise ValueError(f"no tile in {candidates} divides {n}")


def _full_spec(a):
    nd = a.ndim
    return pl.BlockSpec(a.shape, lambda *_: (0,) * nd)


def _row_rms(x, gain_row):
    ms = jnp.mean(x * x, axis=-1, keepdims=True)
    return x * lax.rsqrt(ms + RMS_EPS) * gain_row


def _group_rms(x, group_mean_mat, gain_row):
    ms = jnp.dot((x * x).astype(BF16), group_mean_mat, preferred_element_type=F32)
    return x * lax.rsqrt(ms + RMS_EPS) * gain_row


def _rope_lanes(x, cos_t, sin_t, half):
    rows, n = x.shape
    lane = lax.broadcasted_iota(jnp.int32, (rows, LANES), 1)
    first_half = (lane % (2 * half)) < half
    out = []
    for j in range(n // LANES):
        xb = x[:, j * LANES:(j + 1) * LANES]
        partner_up = pltpu.roll(xb, LANES - half, 1)
        partner_dn = pltpu.roll(xb, half, 1)
        partner = jnp.where(first_half, partner_up, partner_dn)
        out.append(xb * cos_t + partner * sin_t)
    return out[0] if len(out) == 1 else jnp.concatenate(out, axis=1)


def _proj_kernel(x_ref, mod_ref, rope_ref, small_ref,
                 wq_ref, wk_ref, wv_ref, wcq_ref, wckv_ref, wkr_ref, wuv_ref, wg_ref,
                 wuq_ref, wukv_ref, gq_ref, gk_ref,
                 qg_ref, k_ref, vtg_ref, qm_ref, km_ref, vtm_ref, u_ref, vg_ref, g_ref):
    tm = x_ref.shape[1]
    x = x_ref[0]
    mod = mod_ref[0]
    h = _row_rms(x, mod[0:1]) * mod[1:2] + mod[2:3]
    hb = h.astype(BF16)

    cos64, sin64 = rope_ref[0], rope_ref[1]
    cos32, sin32 = rope_ref[2], rope_ref[3]
    cos_kr, sin_kr = rope_ref[4], rope_ref[5]

    small = small_ref[...]
    qn_gain = small[0:1, :GQA_Q_WIDTH]
    kn_gain = small[1:2, :GQA_KV_WIDTH]
    cq_gain = small[2:3, :MLA_Q_RANK]
    ckv_gain = small[3:4, :MLA_KV_RANK]
    vg_gain = small[4:5, :GMLP_WIDTH]

    def proj(w_ref):
        return jnp.dot(hb, w_ref[...], preferred_element_type=F32)

    lane = lax.broadcasted_iota(jnp.int32, (tm, LANES), 1)
    half = LANES // 2
    block = lambda a, j: a[:, j * LANES:(j + 1) * LANES]
    to_lanes = lambda a, shift: a if shift % LANES == 0 else pltpu.roll(a, shift % LANES, 1)

    def store_values_t(v_block, dst_ref, first_head):
        vt = v_block.T
        extra_row = lax.broadcasted_iota(jnp.int32, (VT_EXTRA_ROWS, tm), 0)
        ones_then_zeros = jnp.where(extra_row == 0, 1.0, 0.0).astype(BF16)
        for i in range(2):
            dst_ref[0, first_head + i, 0, 0:half, :] = vt[i * half:(i + 1) * half].astype(BF16)
            dst_ref[0, first_head + i, 0, half:half + VT_EXTRA_ROWS, :] = ones_then_zeros

    q = _group_rms(proj(wq_ref), gq_ref[...], qn_gain)
    q = _rope_lanes(q, cos64, sin64, GQA_HEAD_DIM // 2) * (GQA_HEAD_DIM ** -0.5 * LOG2E)
    group = GQA_HEADS // GQA_KV_HEADS
    for hd in range(GQA_HEADS):
        src_half, dst_half = hd % 2, hd // group
        part = to_lanes(block(q, hd // 2), (dst_half - src_half) * half)
        keep = (lane >= half) if dst_half else (lane < half)
        qg_ref[0, hd] = jnp.where(keep, part, 0.0).astype(BF16)
    k = _group_rms(proj(wk_ref), gk_ref[...], kn_gain)
    k_ref[0] = _rope_lanes(k, cos64, sin64, GQA_HEAD_DIM // 2).astype(BF16)
    store_values_t(proj(wv_ref), vtg_ref, 0)

    cq = _row_rms(proj(wcq_ref), cq_gain).astype(BF16)
    qm = jnp.dot(cq, wuq_ref[...], preferred_element_type=F32)
    mla_scale = (MLA_NOPE_DIM + MLA_ROPE_DIM) ** -0.5 * LOG2E
    qn = qm[:, :MLA_QN_WIDTH] * mla_scale
    qr = _rope_lanes(qm[:, MLA_QN_WIDTH:], cos32, sin32, MLA_ROPE_DIM // 2) * mla_scale
    ckv = _row_rms(proj(wckv_ref), ckv_gain).astype(BF16)
    kv = jnp.dot(ckv, wukv_ref[...], preferred_element_type=F32)
    kn = kv[:, :MLA_QN_WIDTH]
    kr = _rope_lanes(proj(wkr_ref), cos_kr, sin_kr, MLA_ROPE_DIM // 2)
    kr_part = to_lanes(kr, MLA_NOPE_DIM)
    ropes_per_block = LANES // MLA_ROPE_DIM
    in_nope = lane < MLA_NOPE_DIM
    in_rope = lane < MLA_NOPE_DIM + MLA_ROPE_DIM
    for hd in range(MLA_HEADS):
        q_nope = to_lanes(block(qn, hd // 2), -(hd % 2) * half)
        q_rope = to_lanes(block(qr, hd // ropes_per_block),
                          MLA_NOPE_DIM - (hd % ropes_per_block) * MLA_ROPE_DIM)
        qm_ref[0, hd] = jnp.where(in_nope, q_nope, jnp.where(in_rope, q_rope, 0.0)).astype(BF16)
        k_nope = to_lanes(block(kn, hd // 2), -(hd % 2) * half)
        km_ref[0, hd] = jnp.where(in_nope, k_nope, jnp.where(in_rope, kr_part, 0.0)).astype(BF16)
    for j in range(MLA_V_WIDTH // LANES):
        store_values_t(block(kv, MLA_QN_WIDTH // LANES + j), vtm_ref, 2 * j)

    act = jax.nn.gelu(proj(wuv_ref), approximate=True)
    u_ref[0] = act[:, :GMLP_WIDTH].astype(BF16)
    vg_ref[0] = _row_rms(act[:, GMLP_WIDTH:], vg_gain).astype(BF16)

    g_ref[0] = proj(wg_ref).astype(BF16)


def _proj_call(x, mod3, rope, small, weights, tm, kc):
    B, L, D = x.shape
    per_chunk = kc // tm
    dvx = GQA_HEAD_DIM + VT_EXTRA_ROWS
    tok_spec = lambda w: pl.BlockSpec((1, tm, w), lambda b, i: (b, i, 0))
    slab_spec = lambda H: pl.BlockSpec((1, H, tm, LANES), lambda b, i: (b, 0, i, 0))
    vt_spec = lambda H: pl.BlockSpec((1, H, 1, dvx, tm),
                                     lambda b, i: (b, 0, i // per_chunk, 0, i % per_chunk))
    tok_shape = lambda w: jax.ShapeDtypeStruct((B, L, w), BF16)
    slab_shape = lambda H: jax.ShapeDtypeStruct((B, H, L, LANES), BF16)
    vt_shape = lambda H: jax.ShapeDtypeStruct((B, H, L // kc, dvx, kc), BF16)
    in_specs = [tok_spec(D),
                pl.BlockSpec((1,) + mod3.shape[1:], lambda b, i: (b, 0, 0)),
                pl.BlockSpec((rope.shape[0], tm, LANES), lambda b, i: (0, i, 0)),
                _full_spec(small)] + [_full_spec(w) for w in weights]
    names = ("qg", "k", "vtg", "qm", "km", "vtm", "u", "vg", "gates")
    out_specs = [slab_spec(GQA_HEADS), tok_spec(GQA_KV_WIDTH), vt_spec(GQA_KV_HEADS),
                 slab_spec(MLA_HEADS), slab_spec(MLA_HEADS), vt_spec(MLA_HEADS),
                 tok_spec(GMLP_WIDTH), tok_spec(GMLP_WIDTH), tok_spec(N_BRANCHES * D)]
    out_shape = [slab_shape(GQA_HEADS), tok_shape(GQA_KV_WIDTH), vt_shape(GQA_KV_HEADS),
                 slab_shape(MLA_HEADS), slab_shape(MLA_HEADS), vt_shape(MLA_HEADS),
                 tok_shape(GMLP_WIDTH), tok_shape(GMLP_WIDTH), tok_shape(N_BRANCHES * D)]
    outs = pl.pallas_call(
        _proj_kernel,
        grid=(B, L // tm),
        in_specs=in_specs,
        out_specs=out_specs,
        out_shape=out_shape,
        compiler_params=pltpu.CompilerParams(
            dimension_semantics=("parallel", "parallel"), vmem_limit_bytes=VMEM_LIMIT_BYTES),
        name="proj",
    )(x, mod3, rope, small, *weights)
    return dict(zip(names, outs))


def _attn_kernel(*refs, n_main, has_tail):
    if has_tail:
        (q_ref, k_ref, vt_ref, kt_ref, vtt_ref, o_ref,
         sa_ref, sb_ref, ca_ref, cb_ref, m_ref, acc_ref) = refs
    else:
        q_ref, k_ref, vt_ref, o_ref, sa_ref, sb_ref, ca_ref, cb_ref, m_ref, acc_ref = refs
        kt_ref = vtt_ref = None
    bufs = ((sa_ref, ca_ref), (sb_ref, cb_ref))
    dv = o_ref.shape[2]
    q_t = q_ref[0, 0].astype(F32).T.astype(BF16)

    m_ref[...] = jnp.full(m_ref.shape, NEG_BIG, F32)
    acc_ref[...] = jnp.zeros(acc_ref.shape, F32)

    def scores(k_chunk, dst):
        st_ref, cmax_ref = dst
        st = jnp.dot(k_chunk, q_t, preferred_element_type=F32)
        st_ref[0:k_chunk.shape[0], :] = st
        cmax_ref[...] = jnp.max(st, axis=0, keepdims=True)

    def absorb(src, vt_chunk):
        st_ref, cmax_ref = src
        m = m_ref[...]
        m_new = jnp.maximum(m, cmax_ref[...])
        alpha = jnp.exp2(m - m_new)
        pt = jnp.exp2(st_ref[0:vt_chunk.shape[1], :] - m_new).astype(BF16)
        acc_ref[...] = alpha * acc_ref[...] + jnp.dot(vt_chunk, pt, preferred_element_type=F32)
        m_ref[...] = m_new

    main_k = lambda c: k_ref[0, 0, c]
    main_vt = lambda c: vt_ref[0, 0, c]

    scores(main_k(0), bufs[0])
    group = ATTN_LOOP_CHUNKS
    n_groups = (n_main - 1) // group
    if n_groups:
        def run_group(g, carry):
            for j in range(group):
                c = group * g + j
                scores(main_k(c + 1), bufs[(j + 1) % 2])
                absorb(bufs[j % 2], main_vt(c))
            return carry
        lax.fori_loop(0, n_groups, run_group, 0)
    rest = [(main_k, main_vt, c) for c in range(group * n_groups, n_main)]
    if has_tail:
        rest.append((lambda _: kt_ref[0, 0], lambda _: vtt_ref[0, 0], 0))
    for j, (k_of, vt_of, c) in enumerate(rest):
        if j + 1 < len(rest):
            nk_of, _, nc = rest[j + 1]
            scores(nk_of(nc), bufs[(j + 1) % 2])
        absorb(bufs[j % 2], vt_of(c))

    acc = acc_ref[...]
    o_ref[0, 0] = (acc[0:dv] / acc[dv:dv + 1]).astype(o_ref.dtype)


def _attn_call(q, k, vt, k_tail, vt_tail, k_head_of, v_head_of, tq):
    B, H, Lq, _ = q.shape
    _, _, n_main, kc, _ = k.shape
    dvx = vt.shape[3]
    dv = dvx - VT_EXTRA_ROWS
    has_tail = k_tail is not None
    in_specs = [pl.BlockSpec((1, 1, tq, LANES), lambda b, h, i: (b, h, i, 0)),
                pl.BlockSpec((1, 1, n_main, kc, LANES), lambda b, h, i: (b, k_head_of(h), 0, 0, 0)),
                pl.BlockSpec((1, 1, n_main, dvx, kc), lambda b, h, i: (b, v_head_of(h), 0, 0, 0))]
    args = [q, k, vt]
    if has_tail:
        kt = k_tail.shape[2]
        assert kt <= kc
        in_specs += [pl.BlockSpec((1, 1, kt, LANES), lambda b, h, i: (b, k_head_of(h), 0, 0)),
                     pl.BlockSpec((1, 1, dvx, kt), lambda b, h, i: (b, v_head_of(h), 0, 0))]
        args += [k_tail, vt_tail]
    return pl.pallas_call(
        functools.partial(_attn_kernel, n_main=n_main, has_tail=has_tail),
        grid=(B, H, Lq // tq),
        in_specs=in_specs,
        out_specs=pl.BlockSpec((1, 1, dv, tq), lambda b, h, i: (b, h, 0, i)),
        out_shape=jax.ShapeDtypeStruct((B, H, dv, Lq), BF16),
        scratch_shapes=[pltpu.VMEM((kc, tq), F32), pltpu.VMEM((kc, tq), F32),
                        pltpu.VMEM((1, tq), F32), pltpu.VMEM((1, tq), F32),
                        pltpu.VMEM((1, tq), F32), pltpu.VMEM((dvx, tq), F32)],
        compiler_params=pltpu.CompilerParams(
            dimension_semantics=("parallel", "parallel", "parallel"),
            vmem_limit_bytes=VMEM_LIMIT_BYTES),
        name="attn",
    )(*args)


def _merge_kernel(x_ref, mod_ref, ga_ref, ma_ref, u_ref, vg_ref, g_ref,
                  ws_ref, bs_ref, wb_ref, wo_ref, rw_ref, rb_ref,
                  xo_ref, tok_ref, lg_ref):
    tm = x_ref.shape[1]
    D = x_ref.shape[2]
    mod = mod_ref[0]

    lane = lax.broadcasted_iota(jnp.int32, (GMLP_CHUNK, LANES), 1)
    low_group = lane < GMLP_GROUP_DIM
    chunks = []
    for c in range(tm // GMLP_CHUNK):
        rows = slice(c * GMLP_CHUNK, (c + 1) * GMLP_CHUNK)
        cols = []
        for j in range(GMLP_WIDTH // LANES):
            vc = vg_ref[0, rows, j * LANES:(j + 1) * LANES]
            s_lo = jnp.dot(ws_ref[2 * j], vc, preferred_element_type=F32)
            s_hi = jnp.dot(ws_ref[2 * j + 1], vc, preferred_element_type=F32)
            cols.append(jnp.where(low_group, s_lo, s_hi))
        s = jnp.concatenate(cols, axis=1) + bs_ref[...]
        chunks.append((u_ref[0, rows, :].astype(F32) * s).astype(BF16))
    gm = chunks[0] if len(chunks) == 1 else jnp.concatenate(chunks, axis=0)

    branches = (ga_ref[0], ma_ref[0], gm)
    merged = None
    for i in range(N_BRANCHES):
        gate = jax.nn.sigmoid(g_ref[0, :, i * D:(i + 1) * D].astype(F32))
        term = gate * jnp.dot(branches[i], wb_ref[i], preferred_element_type=F32)
        merged = term if merged is None else merged + term
    y = jnp.dot(merged.astype(BF16), wo_ref[...], preferred_element_type=F32)
    x_new = x_ref[0] + mod[1:2] * _row_rms(y, mod[0:1])
    xo_ref[0] = x_new

    tok = (_row_rms(x_new, mod[2:3]) * mod[3:4] + mod[4:5]).astype(BF16)
    tok_ref[0] = tok
    lg_ref[0] = jnp.dot(tok, rw_ref[...], preferred_element_type=F32) + rb_ref[...]


def _merge_call(x, mod5, ga, ma, u, vg, g, ws, bs, wb, wo, rw, rb, tm):
    B, L, D = x.shape
    tok_spec = lambda w: pl.BlockSpec((1, tm, w), lambda b, i: (b, i, 0))
    consts = (ws, bs, wb, wo, rw, rb)
    return pl.pallas_call(
        _merge_kernel,
        grid=(B, L // tm),
        in_specs=[tok_spec(D), pl.BlockSpec((1,) + mod5.shape[1:], lambda b, i: (b, 0, 0)),
                  tok_spec(GQA_Q_WIDTH), tok_spec(MLA_V_WIDTH), tok_spec(GMLP_WIDTH),
                  tok_spec(GMLP_WIDTH), tok_spec(N_BRANCHES * D)] + [_full_spec(a) for a in consts],
        out_specs=[tok_spec(D), tok_spec(D), tok_spec(LANES)],
        out_shape=[jax.ShapeDtypeStruct((B, L, D), F32), jax.ShapeDtypeStruct((B, L, D), BF16),
                   jax.ShapeDtypeStruct((B, L, LANES), F32)],
        compiler_params=pltpu.CompilerParams(
            dimension_semantics=("parallel", "parallel"), vmem_limit_bytes=VMEM_LIMIT_BYTES),
        name="merge",
    )(x, mod5, ga, ma, u, vg, g, *consts)


def _expert_kernel(blk_expert_ref, n_used_ref, xb_ref, w1_ref, b1_ref, w2_ref, b2_ref, y_ref,
                   w1b_ref, w2b_ref):
    i = pl.program_id(0)
    d_expert = w2_ref.shape[1]
    new_expert = jnp.logical_or(i == 0, blk_expert_ref[i] != blk_expert_ref[jnp.maximum(i - 1, 0)])

    @pl.when(new_expert)
    def _():
        w1b_ref[...] = w1_ref[0].astype(BF16)
        w2b_ref[...] = w2_ref[0].astype(BF16)

    @pl.when(i < n_used_ref[0])
    def _():
        a = jnp.dot(xb_ref[...], w1b_ref[...], preferred_element_type=F32) + b1_ref[0]
        glu = jnp.minimum(a[:, :d_expert], SWIGLU_LIMIT)
        lin = jnp.clip(a[:, d_expert:], -SWIGLU_LIMIT, SWIGLU_LIMIT)
        act = glu * jax.nn.sigmoid(SWIGLU_ALPHA * glu) * (lin + 1.0)
        y_ref[...] = jnp.dot(act.astype(BF16), w2b_ref[...], preferred_element_type=F32) + b2_ref[0]

    @pl.when(i >= n_used_ref[0])
    def _():
        y_ref[...] = jnp.zeros(y_ref.shape, y_ref.dtype)


def _expert_call(blk_expert, n_used, xb, w1, b1, w2, b2, layer, rows_per_block):
    n_rows, D = xb.shape
    depth, E, _, two_de = w1.shape
    d_expert = w2.shape[2]
    grid_spec = pltpu.PrefetchScalarGridSpec(
        num_scalar_prefetch=2,
        grid=(n_rows // rows_per_block,),
        in_specs=[pl.BlockSpec((rows_per_block, D), lambda i, be, nu: (i, 0)),
                  pl.BlockSpec((None, 1, D, two_de), lambda i, be, nu: (layer, be[i], 0, 0)),
                  pl.BlockSpec((None, 1, 1, two_de), lambda i, be, nu: (layer, be[i], 0, 0)),
                  pl.BlockSpec((None, 1, d_expert, D), lambda i, be, nu: (layer, be[i], 0, 0)),
                  pl.BlockSpec((None, 1, 1, D), lambda i, be, nu: (layer, be[i], 0, 0))],
        out_specs=pl.BlockSpec((rows_per_block, D), lambda i, be, nu: (i, 0)),
        scratch_shapes=[pltpu.VMEM((D, two_de), BF16), pltpu.VMEM((d_expert, D), BF16)],
    )
    return pl.pallas_call(
        _expert_kernel,
        grid_spec=grid_spec,
        out_shape=jax.ShapeDtypeStruct((n_rows, D), F32),
        compiler_params=pltpu.CompilerParams(
            dimension_semantics=("arbitrary",), vmem_limit_bytes=VMEM_LIMIT_BYTES),
        name="experts",
    )(blk_expert, n_used, xb, w1, b1.reshape(depth, E, 1, two_de), w2, b2.reshape(depth, E, 1, D))


def _combine_kernel(dcur_ref, dnxt_ref, y_hbm, w_ref, x_ref, mod_ref, o_ref, gbuf, sem, *, n_tiles):
    s = pl.program_id(0)
    last = n_tiles - 1
    tm = x_ref.shape[0]
    slot = s % 2

    def row_copy(row, k, r, slot_):
        return pltpu.make_async_copy(y_hbm.at[pl.ds(row, 1)], gbuf.at[slot_, k, pl.ds(r, 1)],
                                     sem.at[slot_])

    def start_gather(d_ref, slot_):
        def token(r, carry):
            for k in range(TOP_K):
                row_copy(d_ref[0, 0, r * TOP_K + k], k, r, slot_).start()
            return carry
        lax.fori_loop(0, tm, token, 0, unroll=8)

    @pl.when(s == 0)
    def _():
        start_gather(dcur_ref, 0)

    @pl.when(s < last)
    def _():
        start_gather(dnxt_ref, 1 - slot)

    for k in range(TOP_K):
        pltpu.make_async_copy(y_hbm.at[pl.ds(0, tm)], gbuf.at[slot, k], sem.at[slot]).wait()

    w = w_ref[...]
    f = w[:, 0:1] * gbuf[slot, 0]
    for k in range(1, TOP_K):
        f = f + w[:, k:k + 1] * gbuf[slot, k]
    mod = mod_ref[0]
    o_ref[...] = x_ref[...] + mod[1:2] * _row_rms(f, mod[0:1])


def _combine_call(x, y, dest, top_w, mod2, tm):
    B, L, D = x.shape
    n_tiles = B * L // tm
    tiles_per_mod = n_tiles // mod2.shape[0]
    dest3 = dest.reshape(n_tiles, 1, tm * TOP_K)
    tok_spec = pl.BlockSpec((tm, D), lambda s: (s, 0))
    smem_spec = lambda idx: pl.BlockSpec((1, 1, tm * TOP_K), idx, memory_space=pltpu.SMEM)
    out = pl.pallas_call(
        functools.partial(_combine_kernel, n_tiles=n_tiles),
        grid=(n_tiles,),
        in_specs=[smem_spec(lambda s: (s, 0, 0)),
                  smem_spec(lambda s: (jnp.minimum(s + 1, n_tiles - 1), 0, 0)),
                  pl.BlockSpec(memory_space=pl.ANY),
                  pl.BlockSpec((tm, TOP_K), lambda s: (s, 0)),
                  tok_spec,
                  pl.BlockSpec((1,) + mod2.shape[1:], lambda s: (s // tiles_per_mod, 0, 0))],
        out_specs=tok_spec,
        out_shape=jax.ShapeDtypeStruct((B * L, D), F32),
        scratch_shapes=[pltpu.VMEM((2, TOP_K, tm, D), y.dtype), pltpu.SemaphoreType.DMA((2,))],
        compiler_params=pltpu.CompilerParams(
            dimension_semantics=("arbitrary",), vmem_limit_bytes=VMEM_LIMIT_BYTES),
        name="combine",
    )(dest3, dest3, y, top_w, x.reshape(B * L, D), mod2)
    return out.reshape(B, L, D)


def _rope_tables(n_lat, dtype=F32):
    rows = n_lat // GRID_W
    row = jnp.repeat(jnp.arange(rows, dtype=F32), GRID_W)
    col = jnp.tile(jnp.arange(GRID_W, dtype=F32), rows)

    def tables(rot_dim):
        quarter = rot_dim // 4
        inv_freq = ROPE_THETA ** (-jnp.arange(quarter, dtype=F32) / quarter)
        ang = jnp.concatenate([row[:, None] * inv_freq, col[:, None] * inv_freq], axis=-1)
        cos = jnp.concatenate([jnp.cos(ang), jnp.cos(ang)], axis=-1)
        sin = jnp.concatenate([-jnp.sin(ang), jnp.sin(ang)], axis=-1)
        return cos, sin

    cos64, sin64 = tables(GQA_HEAD_DIM)
    cos32, sin32 = tables(MLA_ROPE_DIM)
    tile = lambda t: jnp.tile(t, (1, LANES // t.shape[1]))
    pad_one = lambda t: jnp.concatenate([t, jnp.ones((n_lat, LANES - t.shape[1]), F32)], axis=1)
    pad_zero = lambda t: jnp.concatenate([t, jnp.zeros((n_lat, LANES - t.shape[1]), F32)], axis=1)
    return jnp.stack([tile(cos64), tile(sin64), tile(cos32), tile(sin32),
                      pad_one(cos32), pad_zero(sin32)]).astype(dtype)


def _identity_rope(n):
    one, zero = jnp.ones((n, LANES), F32), jnp.zeros((n, LANES), F32)
    return jnp.stack([one, zero, one, zero, one, zero])


def _group_mean_matrix(width, group):
    idx = np.arange(width) // group
    return jnp.asarray((idx[:, None] == idx[None, :]).astype(np.float32) / group, dtype=BF16)


def _layer_weights(w_in, mla_w_uq, mla_w_ukv, D):
    splits = np.cumsum((GQA_Q_WIDTH, GQA_KV_WIDTH, GQA_KV_WIDTH, MLA_Q_RANK, MLA_KV_RANK, MLA_ROPE_DIM,
                        2 * GMLP_WIDTH, N_BRANCHES * D))[:-1].tolist()
    wq, wk, wv, wcq, wckv, wkr, wuv, wg = jnp.split(w_in.astype(BF16), splits, axis=1)
    wkr = jnp.pad(wkr, ((0, 0), (0, LANES - MLA_ROPE_DIM)))
    uq = mla_w_uq.astype(BF16).reshape(MLA_Q_RANK, MLA_HEADS, MLA_NOPE_DIM + MLA_ROPE_DIM)
    wuq = jnp.concatenate([uq[:, :, :MLA_NOPE_DIM].reshape(MLA_Q_RANK, -1),
                           uq[:, :, MLA_NOPE_DIM:].reshape(MLA_Q_RANK, -1)], axis=1)
    ukv = mla_w_ukv.astype(BF16).reshape(MLA_KV_RANK, MLA_HEADS, MLA_NOPE_DIM + MLA_V_DIM)
    wukv = jnp.concatenate([ukv[:, :, :MLA_NOPE_DIM].reshape(MLA_KV_RANK, -1),
                            ukv[:, :, MLA_NOPE_DIM:].reshape(MLA_KV_RANK, -1)], axis=1)
    return (wq, wk, wv, wcq, wckv, wkr, wuv, wg, wuq, wukv,
            _group_mean_matrix(GQA_Q_WIDTH, GQA_HEAD_DIM), _group_mean_matrix(GQA_KV_WIDTH, GQA_HEAD_DIM))


def _small_params(qk_norm, mla_q_norm, mla_kv_norm, gmlp_v_norm):
    width = max(GQA_Q_WIDTH, GMLP_WIDTH)
    row = lambda v: jnp.pad(v, (0, width - v.shape[0]))
    rows = [row(jnp.tile(qk_norm[0], GQA_HEADS)), row(jnp.tile(qk_norm[1], GQA_KV_HEADS)),
            row(mla_q_norm), row(mla_kv_norm), row(gmlp_v_norm)]
    rows += [jnp.zeros((width,), F32)] * (8 - len(rows))
    return jnp.stack(rows).astype(F32)


def _stack_rows(rows, B, D):
    full = [jnp.broadcast_to(r, (B, D)) for r in rows]
    full += [jnp.zeros((B, D), F32)] * (8 - len(full))
    return jnp.stack(full, axis=1).astype(F32)


def _heads_last(ot, B, Lx):
    return jnp.transpose(ot, (0, 3, 1, 2)).reshape(B, Lx, ot.shape[1] * ot.shape[2])


def _attention_pair(queries, main, tail, tq):
    B, _, Lq, _ = queries["qg"].shape
    n, kc = main["vtg"].shape[2], main["vtg"].shape[4]
    group = GQA_HEADS // GQA_KV_HEADS
    chunked = lambda k: k.reshape(B, k.shape[1], n, kc, LANES)
    whole = lambda vt: vt.reshape(B, vt.shape[1], vt.shape[3], vt.shape[4])
    gqa_tail = (None, None) if tail is None else (tail["k"][:, None], whole(tail["vtg"]))
    mla_tail = (None, None) if tail is None else (tail["km"], whole(tail["vtm"]))
    gqa = _attn_call(queries["qg"], chunked(main["k"][:, None]), main["vtg"], *gqa_tail,
                     lambda h: 0, lambda h: h // group, tq)
    mla = _attn_call(queries["qm"], chunked(main["km"]), main["vtm"], *mla_tail,
                     lambda h: h, lambda h: h, tq)
    return _heads_last(gqa, B, Lq), _heads_last(mla, B, Lq)


def _route(logits, n_experts, rows_per_block):
    T = logits.shape[0]
    P = T * TOP_K
    top_logit, top_idx = lax.top_k(logits, TOP_K)
    top_w = jax.nn.softmax(top_logit, axis=-1)
    e_flat = top_idx.reshape(P)
    onehot = (e_flat[:, None] == jnp.arange(n_experts, dtype=e_flat.dtype)[None, :]).astype(jnp.int32)
    csum = jnp.cumsum(onehot, axis=0)
    rank = jnp.sum(onehot * (csum - 1), axis=1)
    counts = csum[-1]
    padded = (counts + rows_per_block - 1) // rows_per_block * rows_per_block
    pad_end = jnp.cumsum(padded)
    pad_start = pad_end - padded
    dest = pad_start[e_flat] + rank
    n_blocks = -(-(P + n_experts * (rows_per_block - 1)) // rows_per_block)
    blk_start = jnp.arange(n_blocks, dtype=jnp.int32) * rows_per_block
    blk_expert = jnp.minimum(jnp.sum((pad_end[None, :] <= blk_start[:, None]).astype(jnp.int32), axis=1),
                             n_experts - 1)
    n_used = (pad_end[-1] // rows_per_block).astype(jnp.int32).reshape(1)
    bits = max(1, (P - 1).bit_length())
    pair_sorted = jnp.sort((e_flat.astype(jnp.int32) << bits) | jnp.arange(P, dtype=jnp.int32)) & ((1 << bits) - 1)
    e_row = jnp.repeat(blk_expert, rows_per_block)
    row = jnp.arange(n_blocks * rows_per_block, dtype=jnp.int32)
    within = row - pad_start[e_row]
    live = within < counts[e_row]
    grp_start = jnp.cumsum(counts) - counts
    src = jnp.where(live, grp_start[e_row] + within, row % P)
    row_token = jnp.where(live, pair_sorted.at[src].get(mode="promise_in_bounds") // TOP_K, row % T)
    return top_w, dest.reshape(T, TOP_K), row_token, blk_expert, n_used


def _layer(x, xc, mod, mod_c, rope_lat, rope_ctx, p, layer, update_ctx):
    B, L, D = x.shape
    C = xc.shape[1]
    (norms, w_in, qk_norm, mla_q_norm, mla_kv_norm, mla_w_uq, mla_w_ukv, gmlp_v_norm, gmlp_w_s,
     gmlp_b_s, w_branch, w_out, router_w, router_b, ew1, eb1, ew2, eb2) = p
    n_experts = router_w.shape[1]

    weights = _layer_weights(w_in, mla_w_uq, mla_w_ukv, D)
    small = _small_params(qk_norm, mla_q_norm, mla_kv_norm, gmlp_v_norm)
    sh1, sc1, g1, sh2, sc2, g2 = [mod[:, j] for j in range(N_MOD)]
    sh1c, sc1c, g1c, sh2c, sc2c, g2c = [mod_c[j] for j in range(N_MOD)]

    tm_lat = _pick_tile(L, (512, 256, 128))
    tm_ctx = _pick_tile(C, (256, 128))
    kc_lat = _pick_tile(L, (ATTN_CHUNK, 512, 256, 128))
    assert C <= kc_lat, "context keys are absorbed as one tail chunk"
    lat = _proj_call(x, _stack_rows([norms[0], 1.0 + sc1, sh1], B, D), rope_lat, small, weights,
                     tm_lat, kc_lat)
    cx = _proj_call(xc, _stack_rows([norms[0], 1.0 + sc1c, sh1c], B, D), rope_ctx, small, weights,
                    tm_ctx, C)
    tq = _pick_tile(L, (512, 256, 128))
    gqa_o, mla_o = _attention_pair(lat, lat, cx, tq)
    u, vg, gates = lat["u"], lat["vg"], lat["gates"]

    ws = gmlp_w_s.astype(BF16)
    bs = jnp.repeat(gmlp_b_s.T, GMLP_GROUP_DIM, axis=1).astype(F32)
    wb = w_branch.astype(BF16)
    wo = w_out.astype(BF16)
    rw = jnp.pad(router_w, ((0, 0), (0, LANES - n_experts))).astype(BF16)
    rb = jnp.pad(router_b, (0, LANES - n_experts)).reshape(1, LANES).astype(F32)

    tm_merge = _pick_tile(L, (512, 256, 128))
    x1, tok, logits = _merge_call(x, _stack_rows([norms[1], g1, norms[2], 1.0 + sc2, sh2], B, D),
                                  gqa_o, mla_o, u, vg, gates, ws, bs, wb, wo, rw, rb, tm_merge)
    tok_all = tok.reshape(B * L, D)
    logits_all = logits.reshape(B * L, LANES)[:, :n_experts]
    if update_ctx:
        gqa_c, mla_c = _attention_pair(cx, cx, None, tm_ctx)
        xc1, tokc, logitsc = _merge_call(
            xc, _stack_rows([norms[1], g1c, norms[2], 1.0 + sc2c, sh2c], B, D),
            gqa_c, mla_c, cx["u"], cx["vg"], cx["gates"], ws, bs, wb, wo, rw, rb, tm_ctx)
        tok_all = jnp.concatenate([tok_all, tokc.reshape(B * C, D)], axis=0)
        logits_all = jnp.concatenate([logits_all, logitsc.reshape(B * C, LANES)[:, :n_experts]], axis=0)

    rows_per_block = EXPERT_BLOCK_ROWS
    top_w, dest, row_token, blk_expert, n_used = _route(logits_all, n_experts, rows_per_block)
    xb = tok_all.at[row_token].get(mode="promise_in_bounds")
    y = _expert_call(blk_expert, n_used, xb, ew1, eb1, ew2, eb2, layer, rows_per_block)

    x2 = _combine_call(x1, y, dest[:B * L], top_w[:B * L], _stack_rows([norms[3], g2], B, D), tm_merge)
    if update_ctx:
        xc = _combine_call(xc1, y, dest[B * L:], top_w[B * L:], _stack_rows([norms[3], g2c], B, D), tm_ctx)
    return x2, xc


def kernel(x, c, ctx, c_ctx, w_mod, b_mod, norm_gains, w_in, qk_norm, mla_q_norm, mla_kv_norm, mla_w_uq,
           mla_w_ukv, gmlp_v_norm, gmlp_w_s, gmlp_b_s, w_branch, w_out, router_w, router_b, expert_w1,
           expert_b1, expert_w2, expert_b2):
    B, L, D = x.shape
    C = ctx.shape[1]
    depth = w_mod.shape[0]
    rope_lat = _rope_tables(L)
    rope_ctx = _identity_rope(C)
    s_c = jax.nn.silu(c)
    s_cc = jax.nn.silu(c_ctx)
    xc = ctx
    for i in range(depth):
        mod = (jnp.dot(s_c, w_mod[i], precision=lax.Precision.HIGHEST) + b_mod[i]).reshape(B, N_MOD, D)
        mod_c = (jnp.dot(s_cc, w_mod[i], precision=lax.Precision.HIGHEST) + b_mod[i]).reshape(N_MOD, D)
        params = (norm_gains[i], w_in[i], qk_norm[i], mla_q_norm[i], mla_kv_norm[i], mla_w_uq[i],
                  mla_w_ukv[i], gmlp_v_norm[i], gmlp_w_s[i], gmlp_b_s[i], w_branch[i], w_out[i],
                  router_w[i], router_b[i], expert_w1, expert_b1, expert_w2, expert_b2)
        x, xc = _layer(x, xc, mod, mod_c, rope_lat, rope_ctx, params, i, update_ctx=(i < depth - 1))
    return x
```

```python
import functools
import math

import jax
import jax.numpy as jnp
import numpy as np
from jax import lax
from jax.experimental import pallas as pl
from jax.experimental.pallas import tpu as pltpu

GRID_W = 64
ROPE_THETA = 10000.0
RMS_EPS = 1e-6
N_MOD = 6
GQA_HEADS = 8
GQA_KV_HEADS = 2
GQA_HEAD_DIM = 64
MLA_HEADS = 8
MLA_NOPE_DIM = 64
MLA_ROPE_DIM = 32
MLA_V_DIM = 64
MLA_Q_RANK = 256
MLA_KV_RANK = 128
GMLP_GROUPS = 8
GMLP_GROUP_DIM = 64
GMLP_CHUNK = 128
N_BRANCHES = 3
TOP_K = 4
SWIGLU_ALPHA = 1.702
SWIGLU_LIMIT = 7.0

LANES = 128
VMEM_LIMIT_BYTES = 56 * 2**20
NEG_BIG = -1e30
ATTN_CHUNK = 1024
ATTN_LOOP_CHUNKS = 4
VT_EXTRA_ROWS = 16
EXPERT_BLOCK_ROWS = 512
LOG2E = math.log2(math.e)

BF16 = jnp.bfloat16
F32 = jnp.float32

GQA_Q_WIDTH = GQA_HEADS * GQA_HEAD_DIM
GQA_KV_WIDTH = GQA_KV_HEADS * GQA_HEAD_DIM
GMLP_WIDTH = GMLP_GROUPS * GMLP_GROUP_DIM
MLA_QN_WIDTH = MLA_HEADS * MLA_NOPE_DIM
MLA_QR_WIDTH = MLA_HEADS * MLA_ROPE_DIM
MLA_V_WIDTH = MLA_HEADS * MLA_V_DIM


def _pick_tile(n, candidates):
    for t in candidates:
        if n % t == 0:
            return t
    raise ValueError(f"no tile in {candidates} divides {n}")


def _full_spec(a):
    nd = a.ndim
    return pl.BlockSpec(a.shape, lambda *_: (0,) * nd)


def _row_rms(x, gain_row):
    ms = jnp.mean(x * x, axis=-1, keepdims=True)
    return x * lax.rsqrt(ms + RMS_EPS) * gain_row


def _group_rms(x, group_mean_mat, gain_row):
    ms = jnp.dot((x * x).astype(BF16), group_mean_mat, preferred_element_type=F32)
    return x * lax.rsqrt(ms + RMS_EPS) * gain_row


def _rope_lanes(x, cos_t, sin_t, half):
    rows, n = x.shape
    lane = lax.broadcasted_iota(jnp.int32, (rows, LANES), 1)
    first_half = (lane % (2 * half)) < half
    out = []
    for j in range(n // LANES):
        xb = x[:, j * LANES:(j + 1) * LANES]
        partner_up = pltpu.roll(xb, LANES - half, 1)
        partner_dn = pltpu.roll(xb, half, 1)
        partner = jnp.where(first_half, partner_up, partner_dn)
        out.append(xb * cos_t + partner * sin_t)
    return out[0] if len(out) == 1 else jnp.concatenate(out, axis=1)


def _proj_kernel(x_ref, mod_ref, rope_ref, small_ref,
                 wq_ref, wk_ref, wv_ref, wcq_ref, wckv_ref, wkr_ref, wuv_ref, wg_ref,
                 wuq_ref, wukv_ref, gq_ref, gk_ref,
                 qg_ref, k_ref, vtg_ref, qm_ref, km_ref, vtm_ref, u_ref, vg_ref, g_ref):
    tm = x_ref.shape[1]
    x = x_ref[0]
    mod = mod_ref[0]
    h = _row_rms(x, mod[0:1]) * mod[1:2] + mod[2:3]
    hb = h.astype(BF16)

    cos64, sin64 = rope_ref[0], rope_ref[1]
    cos32, sin32 = rope_ref[2], rope_ref[3]
    cos_kr, sin_kr = rope_ref[4], rope_ref[5]

    small = small_ref[...]
    qn_gain = small[0:1, :GQA_Q_WIDTH]
    kn_gain = small[1:2, :GQA_KV_WIDTH]
    cq_gain = small[2:3, :MLA_Q_RANK]
    ckv_gain = small[3:4, :MLA_KV_RANK]
    vg_gain = small[4:5, :GMLP_WIDTH]

    def proj(w_ref):
        return jnp.dot(hb, w_ref[...], preferred_element_type=F32)

    lane = lax.broadcasted_iota(jnp.int32, (tm, LANES), 1)
    half = LANES // 2
    block = lambda a, j: a[:, j * LANES:(j + 1) * LANES]
    to_lanes = lambda a, shift: a if shift % LANES == 0 else pltpu.roll(a, shift % LANES, 1)

    def store_values_t(v_block, dst_ref, first_head):
        vt = v_block.T
        extra_row = lax.broadcasted_iota(jnp.int32, (VT_EXTRA_ROWS, tm), 0)
        ones_then_zeros = jnp.where(extra_row == 0, 1.0, 0.0).astype(BF16)
        for i in range(2):
            dst_ref[0, first_head + i, 0, 0:half, :] = vt[i * half:(i + 1) * half].astype(BF16)
            dst_ref[0, first_head + i, 0, half:half + VT_EXTRA_ROWS, :] = ones_then_zeros

    q = _group_rms(proj(wq_ref), gq_ref[...], qn_gain)
    q = _rope_lanes(q, cos64, sin64, GQA_HEAD_DIM // 2) * (GQA_HEAD_DIM ** -0.5 * LOG2E)
    group = GQA_HEADS // GQA_KV_HEADS
    for hd in range(GQA_HEADS):
        src_half, dst_half = hd % 2, hd // group
        part = to_lanes(block(q, hd // 2), (dst_half - src_half) * half)
        keep = (lane >= half) if dst_half else (lane < half)
        qg_ref[0, hd] = jnp.where(keep, part, 0.0).astype(BF16)
    k = _group_rms(proj(wk_ref), gk_ref[...], kn_gain)
    k_ref[0] = _rope_lanes(k, cos64, sin64, GQA_HEAD_DIM // 2).astype(BF16)
    store_values_t(proj(wv_ref), vtg_ref, 0)

    cq = _row_rms(proj(wcq_ref), cq_gain).astype(BF16)
    qm = jnp.dot(cq, wuq_ref[...], preferred_element_type=F32)
    mla_scale = (MLA_NOPE_DIM + MLA_ROPE_DIM) ** -0.5 * LOG2E
    qn = qm[:, :MLA_QN_WIDTH] * mla_scale
    qr = _rope_lanes(qm[:, MLA_QN_WIDTH:], cos32, sin32, MLA_ROPE_DIM // 2) * mla_scale
    ckv = _row_rms(proj(wckv_ref), ckv_gain).astype(BF16)
    kv = jnp.dot(ckv, wukv_ref[...], preferred_element_type=F32)
    kn = kv[:, :MLA_QN_WIDTH]
    kr = _rope_lanes(proj(wkr_ref), cos_kr, sin_kr, MLA_ROPE_DIM // 2)
    kr_part = to_lanes(kr, MLA_NOPE_DIM)
    ropes_per_block = LANES // MLA_ROPE_DIM
    in_nope = lane < MLA_NOPE_DIM
    in_rope = lane < MLA_NOPE_DIM + MLA_ROPE_DIM
    for hd in range(MLA_HEADS):
        q_nope = to_lanes(block(qn, hd // 2), -(hd % 2) * half)
        q_rope = to_lanes(block(qr, hd // ropes_per_block),
                          MLA_NOPE_DIM - (hd % ropes_per_block) * MLA_ROPE_DIM)
        qm_ref[0, hd] = jnp.where(in_nope, q_nope, jnp.where(in_rope, q_rope, 0.0)).astype(BF16)
        k_nope = to_lanes(block(kn, hd // 2), -(hd % 2) * half)
        km_ref[0, hd] = jnp.where(in_nope, k_nope, jnp.where(in_rope, kr_part, 0.0)).astype(BF16)
    for j in range(MLA_V_WIDTH // LANES):
        store_values_t(block(kv, MLA_QN_WIDTH // LANES + j), vtm_ref, 2 * j)

    act = jax.nn.gelu(proj(wuv_ref), approximate=True)
    u_ref[0] = act[:, :GMLP_WIDTH].astype(BF16)
    vg_ref[0] = _row_rms(act[:, GMLP_WIDTH:], vg_gain).astype(BF16)

    g_ref[0] = proj(wg_ref).astype(BF16)


def _proj_call(x, mod3, rope, small, weights, tm, kc):
    B, L, D = x.shape
    per_chunk = kc // tm
    dvx = GQA_HEAD_DIM + VT_EXTRA_ROWS
    tok_spec = lambda w: pl.BlockSpec((1, tm, w), lambda b, i: (b, i, 0))
    slab_spec = lambda H: pl.BlockSpec((1, H, tm, LANES), lambda b, i: (b, 0, i, 0))
    vt_spec = lambda H: pl.BlockSpec((1, H, 1, dvx, tm),
                                     lambda b, i: (b, 0, i // per_chunk, 0, i % per_chunk))
    tok_shape = lambda w: jax.ShapeDtypeStruct((B, L, w), BF16)
    slab_shape = lambda H: jax.ShapeDtypeStruct((B, H, L, LANES), BF16)
    vt_shape = lambda H: jax.ShapeDtypeStruct((B, H, L // kc, dvx, kc), BF16)
    in_specs = [tok_spec(D),
                pl.BlockSpec((1,) + mod3.shape[1:], lambda b, i: (b, 0, 0)),
                pl.BlockSpec((rope.shape[0], tm, LANES), lambda b, i: (0, i, 0)),
                _full_spec(small)] + [_full_spec(w) for w in weights]
    names = ("qg", "k", "vtg", "qm", "km", "vtm", "u", "vg", "gates")
    out_specs = [slab_spec(GQA_HEADS), tok_spec(GQA_KV_WIDTH), vt_spec(GQA_KV_HEADS),
                 slab_spec(MLA_HEADS), slab_spec(MLA_HEADS), vt_spec(MLA_HEADS),
                 tok_spec(GMLP_WIDTH), tok_spec(GMLP_WIDTH), tok_spec(N_BRANCHES * D)]
    out_shape = [slab_shape(GQA_HEADS), tok_shape(GQA_KV_WIDTH), vt_shape(GQA_KV_HEADS),
                 slab_shape(MLA_HEADS), slab_shape(MLA_HEADS), vt_shape(MLA_HEADS),
                 tok_shape(GMLP_WIDTH), tok_shape(GMLP_WIDTH), tok_shape(N_BRANCHES * D)]
    outs = pl.pallas_call(
        _proj_kernel,
        grid=(B, L // tm),
        in_specs=in_specs,
        out_specs=out_specs,
        out_shape=out_shape,
        compiler_params=pltpu.CompilerParams(
            dimension_semantics=("parallel", "parallel"), vmem_limit_bytes=VMEM_LIMIT_BYTES),
        name="proj",
    )(x, mod3, rope, small, *weights)
    return dict(zip(names, outs))


def _attn_kernel(*refs, n_main, has_tail):
    if has_tail:
        (q_ref, k_ref, vt_ref, kt_ref, vtt_ref, o_ref,
         sa_ref, sb_ref, ca_ref, cb_ref, m_ref, acc_ref) = refs
    else:
        q_ref, k_ref, vt_ref, o_ref, sa_ref, sb_ref, ca_ref, cb_ref, m_ref, acc_ref = refs
        kt_ref = vtt_ref = None
    bufs = ((sa_ref, ca_ref), (sb_ref, cb_ref))
    dv = o_ref.shape[2]
    q_t = q_ref[0, 0].astype(F32).T.astype(BF16)

    m_ref[...] = jnp.full(m_ref.shape, NEG_BIG, F32)
    acc_ref[...] = jnp.zeros(acc_ref.shape, F32)

    def scores(k_chunk, dst):
        st_ref, cmax_ref = dst
        st = jnp.dot(k_chunk, q_t, preferred_element_type=F32)
        st_ref[0:k_chunk.shape[0], :] = st
        cmax_ref[...] = jnp.max(st, axis=0, keepdims=True)

    def absorb(src, vt_chunk):
        st_ref, cmax_ref = src
        m = m_ref[...]
        m_new = jnp.maximum(m, cmax_ref[...])
        alpha = jnp.exp2(m - m_new)
        pt = jnp.exp2(st_ref[0:vt_chunk.shape[1], :] - m_new).astype(BF16)
        acc_ref[...] = alpha * acc_ref[...] + jnp.dot(vt_chunk, pt, preferred_element_type=F32)
        m_ref[...] = m_new

    main_k = lambda c: k_ref[0, 0, c]
    main_vt = lambda c: vt_ref[0, 0, c]

    scores(main_k(0), bufs[0])
    group = ATTN_LOOP_CHUNKS
    n_groups = (n_main - 1) // group
    if n_groups:
        def run_group(g, carry):
            for j in range(group):
                c = group * g + j
                scores(main_k(c + 1), bufs[(j + 1) % 2])
                absorb(bufs[j % 2], main_vt(c))
            return carry
        lax.fori_loop(0, n_groups, run_group, 0)
    rest = [(main_k, main_vt, c) for c in range(group * n_groups, n_main)]
    if has_tail:
        rest.append((lambda _: kt_ref[0, 0], lambda _: vtt_ref[0, 0], 0))
    for j, (k_of, vt_of, c) in enumerate(rest):
        if j + 1 < len(rest):
            nk_of, _, nc = rest[j + 1]
            scores(nk_of(nc), bufs[(j + 1) % 2])
        absorb(bufs[j % 2], vt_of(c))

    acc = acc_ref[...]
    o_ref[0, 0] = (acc[0:dv] / acc[dv:dv + 1]).astype(o_ref.dtype)


def _attn_call(q, k, vt, k_tail, vt_tail, k_head_of, v_head_of, tq):
    B, H, Lq, _ = q.shape
    _, _, n_main, kc, _ = k.shape
    dvx = vt.shape[3]
    dv = dvx - VT_EXTRA_ROWS
    has_tail = k_tail is not None
    in_specs = [pl.BlockSpec((1, 1, tq, LANES), lambda b, h, i: (b, h, i, 0)),
                pl.BlockSpec((1, 1, n_main, kc, LANES), lambda b, h, i: (b, k_head_of(h), 0, 0, 0)),
                pl.BlockSpec((1, 1, n_main, dvx, kc), lambda b, h, i: (b, v_head_of(h), 0, 0, 0))]
    args = [q, k, vt]
    if has_tail:
        kt = k_tail.shape[2]
        assert kt <= kc
        in_specs += [pl.BlockSpec((1, 1, kt, LANES), lambda b, h, i: (b, k_head_of(h), 0, 0)),
                     pl.BlockSpec((1, 1, dvx, kt), lambda b, h, i: (b, v_head_of(h), 0, 0))]
        args += [k_tail, vt_tail]
    return pl.pallas_call(
        functools.partial(_attn_kernel, n_main=n_main, has_tail=has_tail),
        grid=(B, H, Lq // tq),
        in_specs=in_specs,
        out_specs=pl.BlockSpec((1, 1, dv, tq), lambda b, h, i: (b, h, 0, i)),
        out_shape=jax.ShapeDtypeStruct((B, H, dv, Lq), BF16),
        scratch_shapes=[pltpu.VMEM((kc, tq), F32), pltpu.VMEM((kc, tq), F32),
                        pltpu.VMEM((1, tq), F32), pltpu.VMEM((1, tq), F32),
                        pltpu.VMEM((1, tq), F32), pltpu.VMEM((dvx, tq), F32)],
        compiler_params=pltpu.CompilerParams(
            dimension_semantics=("parallel", "parallel", "parallel"),
            vmem_limit_bytes=VMEM_LIMIT_BYTES),
        name="attn",
    )(*args)


def _merge_kernel(x_ref, mod_ref, ga_ref, ma_ref, u_ref, vg_ref, g_ref,
                  ws_ref, bs_ref, wb_ref, wo_ref, rw_ref, rb_ref,
                  xo_ref, tok_ref, lg_ref):
    tm = x_ref.shape[1]
    D = x_ref.shape[2]
    mod = mod_ref[0]

    lane = lax.broadcasted_iota(jnp.int32, (GMLP_CHUNK, LANES), 1)
    low_group = lane < GMLP_GROUP_DIM
    chunks = []
    for c in range(tm // GMLP_CHUNK):
        rows = slice(c * GMLP_CHUNK, (c + 1) * GMLP_CHUNK)
        cols = []
        for j in range(GMLP_WIDTH // LANES):
            vc = vg_ref[0, rows, j * LANES:(j + 1) * LANES]
            s_lo = jnp.dot(ws_ref[2 * j], vc, preferred_element_type=F32)
            s_hi = jnp.dot(ws_ref[2 * j + 1], vc, preferred_element_type=F32)
            cols.append(jnp.where(low_group, s_lo, s_hi))
        s = jnp.concatenate(cols, axis=1) + bs_ref[...]
        chunks.append((u_ref[0, rows, :].astype(F32) * s).astype(BF16))
    gm = chunks[0] if len(chunks) == 1 else jnp.concatenate(chunks, axis=0)

    branches = (ga_ref[0], ma_ref[0], gm)
    merged = None
    for i in range(N_BRANCHES):
        gate = jax.nn.sigmoid(g_ref[0, :, i * D:(i + 1) * D].astype(F32))
        term = gate * jnp.dot(branches[i], wb_ref[i], preferred_element_type=F32)
        merged = term if merged is None else merged + term
    y = jnp.dot(merged.astype(BF16), wo_ref[...], preferred_element_type=F32)
    x_new = x_ref[0] + mod[1:2] * _row_rms(y, mod[0:1])
    xo_ref[0] = x_new

    tok = _row_rms(x_new, mod[2:3]) * mod[3:4] + mod[4:5]
    tok_ref[0] = tok
    lg_ref[0] = jnp.dot(tok.astype(BF16), rw_ref[...], preferred_element_type=F32) + rb_ref[...]


def _merge_call(x, mod5, ga, ma, u, vg, g, ws, bs, wb, wo, rw, rb, tm):
    B, L, D = x.shape
    tok_spec = lambda w: pl.BlockSpec((1, tm, w), lambda b, i: (b, i, 0))
    consts = (ws, bs, wb, wo, rw, rb)
    return pl.pallas_call(
        _merge_kernel,
        grid=(B, L // tm),
        in_specs=[tok_spec(D), pl.BlockSpec((1,) + mod5.shape[1:], lambda b, i: (b, 0, 0)),
                  tok_spec(GQA_Q_WIDTH), tok_spec(MLA_V_WIDTH), tok_spec(GMLP_WIDTH),
                  tok_spec(GMLP_WIDTH), tok_spec(N_BRANCHES * D)] + [_full_spec(a) for a in consts],
        out_specs=[tok_spec(D), tok_spec(D), tok_spec(LANES)],
        out_shape=[jax.ShapeDtypeStruct((B, L, D), F32), jax.ShapeDtypeStruct((B, L, D), F32),
                   jax.ShapeDtypeStruct((B, L, LANES), F32)],
        compiler_params=pltpu.CompilerParams(
            dimension_semantics=("parallel", "parallel"), vmem_limit_bytes=VMEM_LIMIT_BYTES),
        name="merge",
    )(x, mod5, ga, ma, u, vg, g, *consts)


def _dispatch_kernel(rt_ref, n_used_ref, tok_hbm, o_ref, sem):
    rows = o_ref.shape[0]

    @pl.when(pl.program_id(0) < n_used_ref[0])
    def _():
        def row(r, carry):
            pltpu.make_async_copy(tok_hbm.at[pl.ds(rt_ref[0, 0, r], 1)], o_ref.at[pl.ds(r, 1)],
                                  sem.at[0]).start()
            return carry
        lax.fori_loop(0, rows, row, 0, unroll=8)
        pltpu.make_async_copy(tok_hbm.at[pl.ds(0, rows)], o_ref, sem.at[0]).wait()

    @pl.when(pl.program_id(0) >= n_used_ref[0])
    def _():
        o_ref[...] = jnp.zeros(o_ref.shape, o_ref.dtype)


def _dispatch_call(tok, row_token, n_used, rows_per_block):
    T, D = tok.shape
    n_blocks = row_token.shape[0] // rows_per_block
    grid_spec = pltpu.PrefetchScalarGridSpec(
        num_scalar_prefetch=0,
        grid=(n_blocks,),
        in_specs=[pl.BlockSpec((1, 1, rows_per_block), lambda i: (i, 0, 0), memory_space=pltpu.SMEM),
                  pl.BlockSpec(memory_space=pltpu.SMEM),
                  pl.BlockSpec(memory_space=pl.ANY)],
        out_specs=pl.BlockSpec((rows_per_block, D), lambda i: (i, 0)),
        scratch_shapes=[pltpu.SemaphoreType.DMA((1,))],
    )
    return pl.pallas_call(
        _dispatch_kernel,
        grid_spec=grid_spec,
        out_shape=jax.ShapeDtypeStruct((n_blocks * rows_per_block, D), tok.dtype),
        compiler_params=pltpu.CompilerParams(
            dimension_semantics=("arbitrary",), vmem_limit_bytes=VMEM_LIMIT_BYTES),
        name="dispatch",
    )(row_token.reshape(n_blocks, 1, rows_per_block), n_used, tok)


def _expert_kernel(blk_expert_ref, n_used_ref, xb_ref, w1_ref, b1_ref, w2_ref, b2_ref, y_ref,
                   w1b_ref, w2b_ref):
    i = pl.program_id(0)
    d_expert = w2_ref.shape[1]
    new_expert = jnp.logical_or(i == 0, blk_expert_ref[i] != blk_expert_ref[jnp.maximum(i - 1, 0)])

    @pl.when(new_expert)
    def _():
        w1b_ref[...] = w1_ref[0].astype(BF16)
        w2b_ref[...] = w2_ref[0].astype(BF16)

    @pl.when(i < n_used_ref[0])
    def _():
        a = jnp.dot(xb_ref[...].astype(BF16), w1b_ref[...], preferred_element_type=F32) + b1_ref[0]
        glu = jnp.minimum(a[:, :d_expert], SWIGLU_LIMIT)
        lin = jnp.clip(a[:, d_expert:], -SWIGLU_LIMIT, SWIGLU_LIMIT)
        act = glu * jax.nn.sigmoid(SWIGLU_ALPHA * glu) * (lin + 1.0)
        y_ref[...] = jnp.dot(act.astype(BF16), w2b_ref[...], preferred_element_type=F32) + b2_ref[0]

    @pl.when(i >= n_used_ref[0])
    def _():
        y_ref[...] = jnp.zeros(y_ref.shape, y_ref.dtype)


def _expert_call(blk_expert, n_used, xb, w1, b1, w2, b2, layer, rows_per_block):
    n_rows, D = xb.shape
    depth, E, _, two_de = w1.shape
    d_expert = w2.shape[2]
    grid_spec = pltpu.PrefetchScalarGridSpec(
        num_scalar_prefetch=2,
        grid=(n_rows // rows_per_block,),
        in_specs=[pl.BlockSpec((rows_per_block, D), lambda i, be, nu: (i, 0)),
                  pl.BlockSpec((None, 1, D, two_de), lambda i, be, nu: (layer, be[i], 0, 0)),
                  pl.BlockSpec((None, 1, 1, two_de), lambda i, be, nu: (layer, be[i], 0, 0)),
                  pl.BlockSpec((None, 1, d_expert, D), lambda i, be, nu: (layer, be[i], 0, 0)),
                  pl.BlockSpec((None, 1, 1, D), lambda i, be, nu: (layer, be[i], 0, 0))],
        out_specs=pl.BlockSpec((rows_per_block, D), lambda i, be, nu: (i, 0)),
        scratch_shapes=[pltpu.VMEM((D, two_de), BF16), pltpu.VMEM((d_expert, D), BF16)],
    )
    return pl.pallas_call(
        _expert_kernel,
        grid_spec=grid_spec,
        out_shape=jax.ShapeDtypeStruct((n_rows, D), F32),
        compiler_params=pltpu.CompilerParams(
            dimension_semantics=("arbitrary",), vmem_limit_bytes=VMEM_LIMIT_BYTES),
        name="experts",
    )(blk_expert, n_used, xb, w1, b1.reshape(depth, E, 1, two_de), w2, b2.reshape(depth, E, 1, D))


def _combine_kernel(dcur_ref, dnxt_ref, y_hbm, w_ref, x_ref, mod_ref, o_ref, gbuf, sem, *, n_tiles):
    s = pl.program_id(0)
    last = n_tiles - 1
    tm = x_ref.shape[0]
    slot = s % 2

    def row_copy(row, k, r, slot_):
        return pltpu.make_async_copy(y_hbm.at[pl.ds(row, 1)], gbuf.at[slot_, k, pl.ds(r, 1)],
                                     sem.at[slot_])

    def start_gather(d_ref, slot_):
        def token(r, carry):
            for k in range(TOP_K):
                row_copy(d_ref[0, 0, r * TOP_K + k], k, r, slot_).start()
            return carry
        lax.fori_loop(0, tm, token, 0, unroll=8)

    @pl.when(s == 0)
    def _():
        start_gather(dcur_ref, 0)

    @pl.when(s < last)
    def _():
        start_gather(dnxt_ref, 1 - slot)

    for k in range(TOP_K):
        pltpu.make_async_copy(y_hbm.at[pl.ds(0, tm)], gbuf.at[slot, k], sem.at[slot]).wait()

    w = w_ref[...]
    f = w[:, 0:1] * gbuf[slot, 0]
    for k in range(1, TOP_K):
        f = f + w[:, k:k + 1] * gbuf[slot, k]
    mod = mod_ref[0]
    o_ref[...] = x_ref[...] + mod[1:2] * _row_rms(f, mod[0:1])


def _combine_call(x, y, dest, top_w, mod2, tm):
    B, L, D = x.shape
    n_tiles = B * L // tm
    tiles_per_mod = n_tiles // mod2.shape[0]
    dest3 = dest.reshape(n_tiles, 1, tm * TOP_K)
    tok_spec = pl.BlockSpec((tm, D), lambda s: (s, 0))
    smem_spec = lambda idx: pl.BlockSpec((1, 1, tm * TOP_K), idx, memory_space=pltpu.SMEM)
    out = pl.pallas_call(
        functools.partial(_combine_kernel, n_tiles=n_tiles),
        grid=(n_tiles,),
        in_specs=[smem_spec(lambda s: (s, 0, 0)),
                  smem_spec(lambda s: (jnp.minimum(s + 1, n_tiles - 1), 0, 0)),
                  pl.BlockSpec(memory_space=pl.ANY),
                  pl.BlockSpec((tm, TOP_K), lambda s: (s, 0)),
                  tok_spec,
                  pl.BlockSpec((1,) + mod2.shape[1:], lambda s: (s // tiles_per_mod, 0, 0))],
        out_specs=tok_spec,
        out_shape=jax.ShapeDtypeStruct((B * L, D), F32),
        scratch_shapes=[pltpu.VMEM((2, TOP_K, tm, D), y.dtype), pltpu.SemaphoreType.DMA((2,))],
        compiler_params=pltpu.CompilerParams(
            dimension_semantics=("arbitrary",), vmem_limit_bytes=VMEM_LIMIT_BYTES),
        name="combine",
    )(dest3, dest3, y, top_w, x.reshape(B * L, D), mod2)
    return out.reshape(B, L, D)


def _rope_tables(n_lat, dtype=F32):
    rows = n_lat // GRID_W
    row = jnp.repeat(jnp.arange(rows, dtype=F32), GRID_W)
    col = jnp.tile(jnp.arange(GRID_W, dtype=F32), rows)

    def tables(rot_dim):
        quarter = rot_dim // 4
        inv_freq = ROPE_THETA ** (-jnp.arange(quarter, dtype=F32) / quarter)
        ang = jnp.concatenate([row[:, None] * inv_freq, col[:, None] * inv_freq], axis=-1)
        cos = jnp.concatenate([jnp.cos(ang), jnp.cos(ang)], axis=-1)
        sin = jnp.concatenate([-jnp.sin(ang), jnp.sin(ang)], axis=-1)
        return cos, sin

    cos64, sin64 = tables(GQA_HEAD_DIM)
    cos32, sin32 = tables(MLA_ROPE_DIM)
    tile = lambda t: jnp.tile(t, (1, LANES // t.shape[1]))
    pad_one = lambda t: jnp.concatenate([t, jnp.ones((n_lat, LANES - t.shape[1]), F32)], axis=1)
    pad_zero = lambda t: jnp.concatenate([t, jnp.zeros((n_lat, LANES - t.shape[1]), F32)], axis=1)
    return jnp.stack([tile(cos64), tile(sin64), tile(cos32), tile(sin32),
                      pad_one(cos32), pad_zero(sin32)]).astype(dtype)


def _identity_rope(n):
    one, zero = jnp.ones((n, LANES), F32), jnp.zeros((n, LANES), F32)
    return jnp.stack([one, zero, one, zero, one, zero])


def _group_mean_matrix(width, group):
    idx = np.arange(width) // group
    return jnp.asarray((idx[:, None] == idx[None, :]).astype(np.float32) / group, dtype=BF16)


def _layer_weights(w_in, mla_w_uq, mla_w_ukv, D):
    splits = np.cumsum((GQA_Q_WIDTH, GQA_KV_WIDTH, GQA_KV_WIDTH, MLA_Q_RANK, MLA_KV_RANK, MLA_ROPE_DIM,
                        2 * GMLP_WIDTH, N_BRANCHES * D))[:-1].tolist()
    wq, wk, wv, wcq, wckv, wkr, wuv, wg = jnp.split(w_in.astype(BF16), splits, axis=1)
    wkr = jnp.pad(wkr, ((0, 0), (0, LANES - MLA_ROPE_DIM)))
    uq = mla_w_uq.astype(BF16).reshape(MLA_Q_RANK, MLA_HEADS, MLA_NOPE_DIM + MLA_ROPE_DIM)
    wuq = jnp.concatenate([uq[:, :, :MLA_NOPE_DIM].reshape(MLA_Q_RANK, -1),
                           uq[:, :, MLA_NOPE_DIM:].reshape(MLA_Q_RANK, -1)], axis=1)
    ukv = mla_w_ukv.astype(BF16).reshape(MLA_KV_RANK, MLA_HEADS, MLA_NOPE_DIM + MLA_V_DIM)
    wukv = jnp.concatenate([ukv[:, :, :MLA_NOPE_DIM].reshape(MLA_KV_RANK, -1),
                            ukv[:, :, MLA_NOPE_DIM:].reshape(MLA_KV_RANK, -1)], axis=1)
    return (wq, wk, wv, wcq, wckv, wkr, wuv, wg, wuq, wukv,
            _group_mean_matrix(GQA_Q_WIDTH, GQA_HEAD_DIM), _group_mean_matrix(GQA_KV_WIDTH, GQA_HEAD_DIM))


def _small_params(qk_norm, mla_q_norm, mla_kv_norm, gmlp_v_norm):
    width = max(GQA_Q_WIDTH, GMLP_WIDTH)
    row = lambda v: jnp.pad(v, (0, width - v.shape[0]))
    rows = [row(jnp.tile(qk_norm[0], GQA_HEADS)), row(jnp.tile(qk_norm[1], GQA_KV_HEADS)),
            row(mla_q_norm), row(mla_kv_norm), row(gmlp_v_norm)]
    rows += [jnp.zeros((width,), F32)] * (8 - len(rows))
    return jnp.stack(rows).astype(F32)


def _stack_rows(rows, B, D):
    full = [jnp.broadcast_to(r, (B, D)) for r in rows]
    full += [jnp.zeros((B, D), F32)] * (8 - len(full))
    return jnp.stack(full, axis=1).astype(F32)


def _heads_last(ot, B, Lx):
    return jnp.transpose(ot, (0, 3, 1, 2)).reshape(B, Lx, ot.shape[1] * ot.shape[2])


def _attention_pair(queries, main, tail, tq):
    B, _, Lq, _ = queries["qg"].shape
    n, kc = main["vtg"].shape[2], main["vtg"].shape[4]
    group = GQA_HEADS // GQA_KV_HEADS
    chunked = lambda k: k.reshape(B, k.shape[1], n, kc, LANES)
    whole = lambda vt: vt.reshape(B, vt.shape[1], vt.shape[3], vt.shape[4])
    gqa_tail = (None, None) if tail is None else (tail["k"][:, None], whole(tail["vtg"]))
    mla_tail = (None, None) if tail is None else (tail["km"], whole(tail["vtm"]))
    gqa = _attn_call(queries["qg"], chunked(main["k"][:, None]), main["vtg"], *gqa_tail,
                     lambda h: 0, lambda h: h // group, tq)
    mla = _attn_call(queries["qm"], chunked(main["km"]), main["vtm"], *mla_tail,
                     lambda h: h, lambda h: h, tq)
    return _heads_last(gqa, B, Lq), _heads_last(mla, B, Lq)


def _route(logits, n_experts, rows_per_block):
    T = logits.shape[0]
    P = T * TOP_K
    top_logit, top_idx = lax.top_k(logits, TOP_K)
    top_w = jax.nn.softmax(top_logit, axis=-1)
    e_flat = top_idx.reshape(P)
    onehot = (e_flat[:, None] == jnp.arange(n_experts, dtype=e_flat.dtype)[None, :]).astype(jnp.int32)
    csum = jnp.cumsum(onehot, axis=0)
    rank = jnp.sum(onehot * (csum - 1), axis=1)
    counts = csum[-1]
    padded = (counts + rows_per_block - 1) // rows_per_block * rows_per_block
    pad_end = jnp.cumsum(padded)
    pad_start = pad_end - padded
    dest = pad_start[e_flat] + rank
    n_blocks = -(-(P + n_experts * (rows_per_block - 1)) // rows_per_block)
    blk_start = jnp.arange(n_blocks, dtype=jnp.int32) * rows_per_block
    blk_expert = jnp.minimum(jnp.sum((pad_end[None, :] <= blk_start[:, None]).astype(jnp.int32), axis=1),
                             n_experts - 1)
    n_used = (pad_end[-1] // rows_per_block).astype(jnp.int32).reshape(1)
    bits = max(1, (P - 1).bit_length())
    pair_sorted = jnp.sort((e_flat.astype(jnp.int32) << bits) | jnp.arange(P, dtype=jnp.int32)) & ((1 << bits) - 1)
    e_row = jnp.repeat(blk_expert, rows_per_block)
    row = jnp.arange(n_blocks * rows_per_block, dtype=jnp.int32)
    within = row - pad_start[e_row]
    live = within < counts[e_row]
    grp_start = jnp.cumsum(counts) - counts
    src = jnp.where(live, grp_start[e_row] + within, row % P)
    row_token = jnp.where(live, pair_sorted.at[src].get(mode="promise_in_bounds") // TOP_K, row % T)
    return top_w, dest.reshape(T, TOP_K), row_token, blk_expert, n_used


def _layer(x, xc, mod, mod_c, rope_lat, rope_ctx, p, layer, update_ctx):
    B, L, D = x.shape
    C = xc.shape[1]
    (norms, w_in, qk_norm, mla_q_norm, mla_kv_norm, mla_w_uq, mla_w_ukv, gmlp_v_norm, gmlp_w_s,
     gmlp_b_s, w_branch, w_out, router_w, router_b, ew1, eb1, ew2, eb2) = p
    n_experts = router_w.shape[1]

    weights = _layer_weights(w_in, mla_w_uq, mla_w_ukv, D)
    small = _small_params(qk_norm, mla_q_norm, mla_kv_norm, gmlp_v_norm)
    sh1, sc1, g1, sh2, sc2, g2 = [mod[:, j] for j in range(N_MOD)]
    sh1c, sc1c, g1c, sh2c, sc2c, g2c = [mod_c[j] for j in range(N_MOD)]

    tm_lat = _pick_tile(L, (512, 256, 128))
    tm_ctx = _pick_tile(C, (256, 128))
    kc_lat = _pick_tile(L, (ATTN_CHUNK, 512, 256, 128))
    assert C <= kc_lat, "context keys are absorbed as one tail chunk"
    lat = _proj_call(x, _stack_rows([norms[0], 1.0 + sc1, sh1], B, D), rope_lat, small, weights,
                     tm_lat, kc_lat)
    cx = _proj_call(xc, _stack_rows([norms[0], 1.0 + sc1c, sh1c], B, D), rope_ctx, small, weights,
                    tm_ctx, C)
    tq = _pick_tile(L, (512, 256, 128))
    gqa_o, mla_o = _attention_pair(lat, lat, cx, tq)
    u, vg, gates = lat["u"], lat["vg"], lat["gates"]

    ws = gmlp_w_s.astype(BF16)
    bs = jnp.repeat(gmlp_b_s.T, GMLP_GROUP_DIM, axis=1).astype(F32)
    wb = w_branch.astype(BF16)
    wo = w_out.astype(BF16)
    rw = jnp.pad(router_w, ((0, 0), (0, LANES - n_experts))).astype(BF16)
    rb = jnp.pad(router_b, (0, LANES - n_experts)).reshape(1, LANES).astype(F32)

    tm_merge = _pick_tile(L, (512, 256, 128))
    x1, tok, logits = _merge_call(x, _stack_rows([norms[1], g1, norms[2], 1.0 + sc2, sh2], B, D),
                                  gqa_o, mla_o, u, vg, gates, ws, bs, wb, wo, rw, rb, tm_merge)
    tok_all = tok.reshape(B * L, D)
    logits_all = logits.reshape(B * L, LANES)[:, :n_experts]
    if update_ctx:
        gqa_c, mla_c = _attention_pair(cx, cx, None, tm_ctx)
        xc1, tokc, logitsc = _merge_call(
            xc, _stack_rows([norms[1], g1c, norms[2], 1.0 + sc2c, sh2c], B, D),
            gqa_c, mla_c, cx["u"], cx["vg"], cx["gates"], ws, bs, wb, wo, rw, rb, tm_ctx)
        tok_all = jnp.concatenate([tok_all, tokc.reshape(B * C, D)], axis=0)
        logits_all = jnp.concatenate([logits_all, logitsc.reshape(B * C, LANES)[:, :n_experts]], axis=0)

    rows_per_block = EXPERT_BLOCK_ROWS
    top_w, dest, row_token, blk_expert, n_used = _route(logits_all, n_experts, rows_per_block)
    xb = _dispatch_call(tok_all, row_token, n_used, rows_per_block)
    y = _expert_call(blk_expert, n_used, xb, ew1, eb1, ew2, eb2, layer, rows_per_block)

    x2 = _combine_call(x1, y, dest[:B * L], top_w[:B * L], _stack_rows([norms[3], g2], B, D), tm_merge)
    if update_ctx:
        xc = _combine_call(xc1, y, dest[B * L:], top_w[B * L:], _stack_rows([norms[3], g2c], B, D), tm_ctx)
    return x2, xc


def kernel(x, c, ctx, c_ctx, w_mod, b_mod, norm_gains, w_in, qk_norm, mla_q_norm, mla_kv_norm, mla_w_uq,
           mla_w_ukv, gmlp_v_norm, gmlp_w_s, gmlp_b_s, w_branch, w_out, router_w, router_b, expert_w1,
           expert_b1, expert_w2, expert_b2):
    B, L, D = x.shape
    C = ctx.shape[1]
    depth = w_mod.shape[0]
    rope_lat = _rope_tables(L)
    rope_ctx = _identity_rope(C)
    s_c = jax.nn.silu(c)
    s_cc = jax.nn.silu(c_ctx)
    xc = ctx
    for i in range(depth):
        mod = (jnp.dot(s_c, w_mod[i], precision=lax.Precision.HIGHEST) + b_mod[i]).reshape(B, N_MOD, D)
        mod_c = (jnp.dot(s_cc, w_mod[i], precision=lax.Precision.HIGHEST) + b_mod[i]).reshape(N_MOD, D)
        params = (norm_gains[i], w_in[i], qk_norm[i], mla_q_norm[i], mla_kv_norm[i], mla_w_uq[i],
                  mla_w_ukv[i], gmlp_v_norm[i], gmlp_w_s[i], gmlp_b_s[i], w_branch[i], w_out[i],
                  router_w[i], router_b[i], expert_w1, expert_b1, expert_w2, expert_b2)
        x, xc = _layer(x, xc, mod, mod_c, rope_lat, rope_ctx, params, i, update_ctx=(i < depth - 1))
    return x
```
